```python
import math
import jax, jax.numpy as jnp
from jax import lax
import numpy as np

D_MODEL = 1024
BATCH = 4
SEQ = 8192
DEPTH = 1

DA_HEADS = 8
DA_HEAD_DIM = 64
DA_V_DIM = 2 * DA_HEAD_DIM
DA_QK_WIDTH = 2 * DA_HEADS * DA_HEAD_DIM
DA_WIDTH = DA_HEADS * DA_V_DIM
GLA_HEADS = 4
GLA_K_DIM = 128
GLA_V_DIM = 256
GLA_QK_WIDTH = GLA_HEADS * GLA_K_DIM
GLA_WIDTH = GLA_HEADS * GLA_V_DIM
GLA_GATE_RANK = 16
GLA_GATE_NORMALIZER = 16.0
GLA_CHUNK = 64

MIX_WIDTH = DA_WIDTH + GLA_WIDTH
Q_BLOCK = 128
NORM_EPS = 1e-6
IN_SPLITS = (DA_QK_WIDTH, DA_QK_WIDTH, DA_WIDTH, DA_WIDTH,
             GLA_QK_WIDTH, GLA_QK_WIDTH, GLA_WIDTH, GLA_WIDTH, GLA_GATE_RANK)
IN_WIDTH = sum(IN_SPLITS)

kernel_name = "hymba_diffattn_gla_sandwich"


def rms_norm(x, g):
    xf = x.astype(jnp.float32)
    y = xf * lax.rsqrt(jnp.mean(xf * xf, axis=-1, keepdims=True) + NORM_EPS)
    return (y * g.astype(jnp.float32)).astype(x.dtype)


def split_columns(z):
    parts, start = [], 0
    for w in IN_SPLITS:
        parts.append(z[..., start:start + w])
        start += w
    return parts


def diff_attention(q, k, v, lam):
    B, S = q.shape[:2]
    nb = S // Q_BLOCK
    q = q * (DA_HEAD_DIM ** -0.5)
    q_blocks = jnp.moveaxis(q.reshape(B, nb, Q_BLOCK, DA_HEADS, 2, DA_HEAD_DIM), 1, 0)
    key_pos = jnp.arange(S)

    def block(args):
        qb, bi = args
        s = jnp.einsum('bqhmd,bkhmd->bhmqk', qb, k).astype(jnp.float32)
        q_pos = bi * Q_BLOCK + jnp.arange(Q_BLOCK)
        mask = key_pos[None, :] <= q_pos[:, None]
        p = jax.nn.softmax(jnp.where(mask, s, -jnp.inf), axis=-1)
        w = p[:, :, 0] - lam * p[:, :, 1]
        return jnp.einsum('bhqk,bkhv->bqhv', w.astype(v.dtype), v)

    out = lax.map(block, (q_blocks, jnp.arange(nb)))
    return jnp.moveaxis(out, 0, 1).reshape(B, S, DA_HEADS, DA_V_DIM)


def gla_chunked(q, k, v, log_a):
    B, S, H, dk = q.shape
    dv = v.shape[-1]
    C = GLA_CHUNK
    N = S // C
    f32 = jnp.float32

    def chunks(t):
        return t.astype(f32).reshape(B, N, C, H, t.shape[-1]).transpose(0, 3, 1, 2, 4)

    q = chunks(q) * (dk ** -0.5)
    k = chunks(k)
    v = chunks(v)
    b = jnp.cumsum(chunks(log_a), axis=3)
    b_last = b[:, :, :, -1, :]
    q_dec = q * jnp.exp(b)
    k_inv = k * jnp.exp(-b)
    k_tail = k * jnp.exp(b_last[:, :, :, None, :] - b)

    causal = jnp.tril(jnp.ones((C, C), dtype=bool))
    attn = jnp.where(causal, jnp.einsum('bhnik,bhnjk->bhnij', q_dec, k_inv), 0.0)
    o_intra = jnp.einsum('bhnij,bhnjv->bhniv', attn, v)

    def step(state, xs):
        qd, kt, vc, dl = xs
        o = jnp.einsum('bhck,bhkv->bhcv', qd, state)
        state = state * dl[..., None] + jnp.einsum('bhck,bhcv->bhkv', kt, vc)
        return state, o

    xs = (jnp.moveaxis(q_dec, 2, 0), jnp.moveaxis(k_tail, 2, 0),
          jnp.moveaxis(v, 2, 0), jnp.moveaxis(jnp.exp(b_last), 2, 0))
    state0 = jnp.zeros((B, H, dk, dv), f32)
    _, o_inter = lax.scan(step, state0, xs)
    o = o_intra + jnp.moveaxis(o_inter, 0, 2)
    return o.transpose(0, 2, 3, 1, 4).reshape(B, S, H, dv)


def hybrid_layer(x, layer_idx, pre_g, post_g, w_in, w_gk_up, b_gk,
                 lq1, lk1, lq2, lk2, subln_g, gla_g, w_out):
    B, S, _ = x.shape
    h = rms_norm(x, pre_g)
    z = jnp.einsum('bsd,de->bse', h, w_in)
    qa, ka, va, ga, qb, kb, vb, gb, gk_low = split_columns(z)

    lam_init = 0.8 - 0.6 * math.exp(-0.3 * layer_idx)
    lam = (jnp.exp(jnp.sum(lq1.astype(jnp.float32) * lk1.astype(jnp.float32)))
           - jnp.exp(jnp.sum(lq2.astype(jnp.float32) * lk2.astype(jnp.float32)))
           + lam_init)
    oa = diff_attention(qa.reshape(B, S, DA_HEADS, 2, DA_HEAD_DIM),
                        ka.reshape(B, S, DA_HEADS, 2, DA_HEAD_DIM),
                        va.reshape(B, S, DA_HEADS, DA_V_DIM), lam)
    oa = rms_norm(oa, subln_g) * (1.0 - lam_init)
    oa = oa.reshape(B, S, DA_WIDTH) * jax.nn.silu(ga)

    gk = jnp.einsum('bsr,re->bse', gk_low, w_gk_up) + b_gk
    log_a = jax.nn.log_sigmoid(gk.astype(jnp.float32)) / GLA_GATE_NORMALIZER
    ob = gla_chunked(qb.reshape(B, S, GLA_HEADS, GLA_K_DIM),
                     kb.reshape(B, S, GLA_HEADS, GLA_K_DIM),
                     vb.reshape(B, S, GLA_HEADS, GLA_V_DIM),
                     log_a.reshape(B, S, GLA_HEADS, GLA_K_DIM))
    ob = rms_norm(ob, gla_g).astype(x.dtype)
    ob = ob.reshape(B, S, GLA_WIDTH) * jax.nn.silu(gb)

    y = jnp.einsum('bse,ed->bsd', jnp.concatenate([oa, ob], axis=-1), w_out)
    return x + rms_norm(y, post_g)


def setup_inputs(seed: int = 0) -> dict:
    key = jax.random.key(seed)
    ks = jax.random.split(key, 14)
    f32 = jnp.float32
    L = DEPTH
    return {
        "x": jax.random.normal(ks[0], (BATCH, SEQ, D_MODEL), f32),
        "pre_norm_g": 1.0 + 0.02 * jax.random.normal(ks[1], (L, D_MODEL), f32),
        "post_norm_g": 1.0 + 0.02 * jax.random.normal(ks[2], (L, D_MODEL), f32),
        "w_in": jax.random.normal(ks[3], (L, D_MODEL, IN_WIDTH), f32) * D_MODEL ** -0.5,
        "w_gk_up": jax.random.normal(ks[4], (L, GLA_GATE_RANK, GLA_QK_WIDTH), f32) * GLA_GATE_RANK ** -0.5,
        "b_gk": 0.01 * jax.random.normal(ks[5], (L, GLA_QK_WIDTH), f32),
        "lambda_q1": 0.1 * jax.random.normal(ks[6], (L, DA_HEAD_DIM), f32),
        "lambda_k1": 0.1 * jax.random.normal(ks[7], (L, DA_HEAD_DIM), f32),
        "lambda_q2": 0.1 * jax.random.normal(ks[8], (L, DA_HEAD_DIM), f32),
        "lambda_k2": 0.1 * jax.random.normal(ks[9], (L, DA_HEAD_DIM), f32),
        "attn_subln_g": 1.0 + 0.02 * jax.random.normal(ks[10], (L, DA_V_DIM), f32),
        "gla_norm_g": 1.0 + 0.02 * jax.random.normal(ks[11], (L, GLA_V_DIM), f32),
        "w_out": jax.random.normal(ks[12], (L, MIX_WIDTH, D_MODEL), f32) * MIX_WIDTH ** -0.5,
    }


def reference(x, pre_norm_g, post_norm_g, w_in, w_gk_up, b_gk, lambda_q1, lambda_k1,
              lambda_q2, lambda_k2, attn_subln_g, gla_norm_g, w_out):
    for l in range(DEPTH):
        x = hybrid_layer(x, l, pre_norm_g[l], post_norm_g[l], w_in[l], w_gk_up[l], b_gk[l],
                         lambda_q1[l], lambda_k1[l], lambda_q2[l], lambda_k2[l],
                         attn_subln_g[l], gla_norm_g[l], w_out[l])
    return x
```

```python
import functools
import math

import jax
import jax.numpy as jnp
from jax import lax
from jax.experimental import pallas as pl
from jax.experimental.pallas import tpu as pltpu

D_MODEL = 1024
DA_HEADS = 8
DA_HEAD_DIM = 64
DA_V_DIM = 128
DA_WIDTH = DA_HEADS * DA_V_DIM
GLA_HEADS = 4
GLA_K_DIM = 128
GLA_V_DIM = 256
GLA_QK_WIDTH = GLA_HEADS * GLA_K_DIM
GLA_WIDTH = GLA_HEADS * GLA_V_DIM
GLA_GATE_RANK = 16
GLA_GATE_NORMALIZER = 16.0
GLA_CHUNK = 64
NORM_EPS = 1e-6
LAM_INIT = 0.8 - 0.6 * math.exp(-0.3 * 0)

LANE = 128
VMEM_LIMIT = 56 * 1024 * 1024

PROJ_ROWS = 256
ATT_TQ = 512
ATT_TK = PROJ_ROWS
GLA_ROWS = 256
OUT_ROWS = 512
NEG_BIG = -1e30

BF16 = jnp.bfloat16
F32 = jnp.float32


def _silu(x):
    return x * (1.0 / (1.0 + jnp.exp(-x)))


def _log_sigmoid(x):
    return jnp.minimum(x, 0.0) - jnp.log(1.0 + jnp.exp(-jnp.abs(x)))


def _in_proj_kernel(x_ref, g_ref, wt_ref, w_ref, wgk_ref, bgk_ref,
                    qt_ref, vt_ref, k_ref, ga_ref, qb_ref, kb_ref, vb_ref,
                    gb_ref, la_ref):
    x = x_ref[0]
    ms = jnp.mean(x * x, axis=-1, keepdims=True)
    h = (x * lax.rsqrt(ms + NORM_EPS) * g_ref[...]).astype(BF16)

    nt = (((1,), (1,)), ((), ()))
    zt = lax.dot_general(wt_ref[...], h, nt, preferred_element_type=F32)
    q_scale = DA_HEAD_DIM ** -0.5
    for hh in range(DA_HEADS):
        lo = hh * LANE
        qt_ref[0, hh] = (zt[lo:lo + LANE] * q_scale).astype(BF16)
        vt_ref[0, hh, 0] = zt[DA_WIDTH + lo:DA_WIDTH + lo + LANE].astype(BF16)

    def proj(c0, width):
        return jnp.dot(h, w_ref[:, c0:c0 + width], preferred_element_type=F32)

    zk = proj(0, 1024)
    for hh in range(DA_HEADS):
        k_ref[0, hh] = zk[:, hh * LANE:(hh + 1) * LANE].astype(BF16)
    ga_ref[0] = _silu(proj(1024, 1024)).astype(BF16)
    zqk = proj(2048, 1024)
    qb_ref[0] = (zqk[:, :GLA_QK_WIDTH] * (GLA_K_DIM ** -0.5)).astype(BF16)
    kb_ref[0] = zqk[:, GLA_QK_WIDTH:].astype(BF16)
    vb_ref[0] = proj(3072, 1024).astype(BF16)
    gb_ref[0] = _silu(proj(4096, 1024)).astype(BF16)
    gk_low = proj(5120, LANE).astype(BF16)
    gk = jnp.dot(gk_low, wgk_ref[...], preferred_element_type=F32) + bgk_ref[...]
    la_ref[0] = _log_sigmoid(gk) * (1.0 / GLA_GATE_NORMALIZER)


def _in_proj(x, pre_g, w_in, w_gk_up, b_gk):
    B, S, D = x.shape
    tm = PROJ_ROWS
    c = 0
    cols = {}
    for name, width in (("qa", 1024), ("ka", 1024), ("va", 1024), ("ga", 1024),
                        ("qb", 512), ("kb", 512), ("vb", 1024), ("gb", 1024),
                        ("gk", GLA_GATE_RANK)):
        cols[name] = (c, c + width)
        c += width

    def sl(name):
        a, b = cols[name]
        return w_in[:, a:b]

    wt = jnp.concatenate([sl("qa"), sl("va")], axis=1).T.astype(BF16)
    w = jnp.concatenate(
        [sl("ka"), sl("ga"), sl("qb"), sl("kb"), sl("vb"), sl("gb"),
         jnp.pad(sl("gk"), ((0, 0), (0, LANE - GLA_GATE_RANK)))], axis=1).astype(BF16)
    wgk = jnp.pad(w_gk_up, ((0, LANE - GLA_GATE_RANK), (0, 0))).astype(BF16)
    bgk = b_gk.reshape(1, GLA_QK_WIDTH)
    g = pre_g.reshape(1, D)

    const = dict(pipeline_mode=pl.Buffered(1))
    in_specs = [
        pl.BlockSpec((1, tm, D), lambda b, i: (b, i, 0)),
        pl.BlockSpec((1, D), lambda b, i: (0, 0), **const),
        pl.BlockSpec(wt.shape, lambda b, i: (0, 0), **const),
        pl.BlockSpec(w.shape, lambda b, i: (0, 0), **const),
        pl.BlockSpec(wgk.shape, lambda b, i: (0, 0), **const),
        pl.BlockSpec(bgk.shape, lambda b, i: (0, 0), **const),
    ]
    row3 = lambda width: pl.BlockSpec((1, tm, width), lambda b, i: (b, i, 0))
    out_specs = [
        pl.BlockSpec((1, DA_HEADS, LANE, tm), lambda b, i: (b, 0, 0, i)),
        pl.BlockSpec((1, DA_HEADS, 1, LANE, tm), lambda b, i: (b, 0, i, 0, 0)),
        pl.BlockSpec((1, DA_HEADS, tm, LANE), lambda b, i: (b, 0, i, 0)),
        row3(DA_WIDTH),
        row3(GLA_QK_WIDTH), row3(GLA_QK_WIDTH), row3(GLA_WIDTH), row3(GLA_WIDTH),
        row3(GLA_QK_WIDTH),
    ]
    out_shape = [
        jax.ShapeDtypeStruct((B, DA_HEADS, LANE, S), BF16),
        jax.ShapeDtypeStruct((B, DA_HEADS, S // tm, LANE, tm), BF16),
        jax.ShapeDtypeStruct((B, DA_HEADS, S, LANE), BF16),
        jax.ShapeDtypeStruct((B, S, DA_WIDTH), BF16),
        jax.ShapeDtypeStruct((B, S, GLA_QK_WIDTH), BF16),
        jax.ShapeDtypeStruct((B, S, GLA_QK_WIDTH), BF16),
        jax.ShapeDtypeStruct((B, S, GLA_WIDTH), BF16),
        jax.ShapeDtypeStruct((B, S, GLA_WIDTH), BF16),
        jax.ShapeDtypeStruct((B, S, GLA_QK_WIDTH), F32),
    ]
    return pl.pallas_call(
        _in_proj_kernel,
        grid=(B, S // tm),
        in_specs=in_specs,
        out_specs=out_specs,
        out_shape=out_shape,
        compiler_params=pltpu.CompilerParams(
            dimension_semantics=("parallel", "parallel"),
            vmem_limit_bytes=VMEM_LIMIT),
        name="in_proj",
    )(x, g, wt, w, wgk, bgk)


def _diff_attn_kernel(qt_ref, k_ref, vt_ref, gate_ref, lq1_ref, lk1_ref, lq2_ref,
                      lk2_ref, g_ref, o_ref, acc1, acc2, m1, l1, m2, l2):
    tq, tk = ATT_TQ, ATT_TK
    qi = pl.program_id(2)
    qt = qt_ref[0, 0]
    row = lax.broadcasted_iota(jnp.int32, qt.shape, 0)
    zero = jnp.zeros_like(qt)
    q_maps = (jnp.where(row < DA_HEAD_DIM, qt, zero),
              jnp.where(row >= DA_HEAD_DIM, qt, zero))
    stats = ((acc1, m1, l1), (acc2, m2, l2))
    for acc, m, l in stats:
        acc[...] = jnp.zeros_like(acc)
        m[...] = jnp.full_like(m, NEG_BIG)
        l[...] = jnp.zeros_like(l)

    def step(j, mask):
        off = pl.multiple_of(j * tk, tk)
        k = k_ref[0, 0, pl.ds(off, tk), :]
        vt = vt_ref[0, 0, j]
        for qm, (acc, m, l) in zip(q_maps, stats):
            s = jnp.dot(k, qm, preferred_element_type=F32)
            if mask is not None:
                s = jnp.where(mask, s, NEG_BIG)
            m_old = m[...]
            m_new = jnp.maximum(m_old, jnp.max(s, axis=0, keepdims=True))
            alpha = jnp.exp(m_old - m_new)
            p = jnp.exp(s - m_new)
            l[...] = alpha * l[...] + jnp.sum(p, axis=0, keepdims=True)
            acc[...] = alpha * acc[...] + jnp.dot(
                vt, p.astype(BF16), preferred_element_type=F32)
            m[...] = m_new

    sub = tq // tk
    n_full = qi * sub

    def body(j, carry):
        step(j, None)
        return carry

    lax.fori_loop(0, n_full, body, 0)
    key_rel = lax.broadcasted_iota(jnp.int32, (tk, tq), 0)
    qry_rel = lax.broadcasted_iota(jnp.int32, (tk, tq), 1)
    for jj in range(sub):
        step(n_full + jj, key_rel + jj * tk <= qry_rel)

    lam = (jnp.exp(jnp.sum(lq1_ref[...] * lk1_ref[...], axis=-1, keepdims=True))
           - jnp.exp(jnp.sum(lq2_ref[...] * lk2_ref[...], axis=-1, keepdims=True))
           + LAM_INIT)
    o = acc1[...] * (1.0 / l1[...]) - acc2[...] * (lam / l2[...])
    ms = jnp.mean(o * o, axis=0, keepdims=True)
    y = o * lax.rsqrt(ms + NORM_EPS) * g_ref[...] * (1.0 - LAM_INIT)
    o_ref[0] = (y.T * gate_ref[0].astype(F32)).astype(BF16)


def _diff_attn(qt, k, vt, gate, lq1, lk1, lq2, lk2, subln_g):
    B, H, _, S = qt.shape
    tq = ATT_TQ
    lam_spec = pl.BlockSpec((1, DA_HEAD_DIM), lambda b, h, i: (0, 0))
    return pl.pallas_call(
        _diff_attn_kernel,
        grid=(B, H, S // tq),
        in_specs=[
            pl.BlockSpec((1, 1, LANE, tq), lambda b, h, i: (b, h, 0, i)),
            pl.BlockSpec((1, 1, S, LANE), lambda b, h, i: (b, h, 0, 0)),
            pl.BlockSpec((1, 1, S // ATT_TK, LANE, ATT_TK), lambda b, h, i: (b, h, 0, 0, 0)),
            pl.BlockSpec((1, tq, LANE), lambda b, h, i: (b, i, h)),
            lam_spec, lam_spec, lam_spec, lam_spec,
            pl.BlockSpec((DA_V_DIM, 1), lambda b, h, i: (0, 0)),
        ],
        out_specs=pl.BlockSpec((1, tq, LANE), lambda b, h, i: (b, i, h)),
        out_shape=jax.ShapeDtypeStruct((B, S, DA_WIDTH), BF16),
        scratch_shapes=[
            pltpu.VMEM((DA_V_DIM, tq), F32), pltpu.VMEM((DA_V_DIM, tq), F32),
            pltpu.VMEM((1, tq), F32), pltpu.VMEM((1, tq), F32),
            pltpu.VMEM((1, tq), F32), pltpu.VMEM((1, tq), F32),
        ],
        compiler_params=pltpu.CompilerParams(
            dimension_semantics=("parallel", "parallel", "arbitrary"),
            vmem_limit_bytes=VMEM_LIMIT),
        name="diff_attn",
    )(qt, k, vt, gate, lq1.reshape(1, -1), lk1.reshape(1, -1), lq2.reshape(1, -1),
      lk2.reshape(1, -1), subln_g.reshape(DA_V_DIM, 1))


def _gla_kernel(q_ref, k_ref, v_ref, la_ref, gate_ref, g_ref, o_ref, state_ref):
    C = GLA_CHUNK

    @pl.when(pl.program_id(1) == 0)
    def _():
        state_ref[...] = jnp.zeros_like(state_ref)

    r = lax.broadcasted_iota(jnp.int32, (C, C), 0)
    c = lax.broadcasted_iota(jnp.int32, (C, C), 1)
    causal = c <= r
    tri = jnp.where(causal, 1.0, 0.0).astype(BF16)
    tn = (((0,), (0,)), ((), ()))
    nt = (((1,), (1,)), ((), ()))
    for ci in range(GLA_ROWS // C):
        rows = slice(ci * C, (ci + 1) * C)
        for h in range(GLA_HEADS):
            ks = slice(h * GLA_K_DIM, (h + 1) * GLA_K_DIM)
            vs = slice(h * GLA_V_DIM, (h + 1) * GLA_V_DIM)
            la = la_ref[0, rows, ks]
            la_hi = la.astype(BF16)
            la_lo = (la - la_hi.astype(F32)).astype(BF16)
            b = (jnp.dot(tri, la_hi, preferred_element_type=F32)
                 + jnp.dot(tri, la_lo, preferred_element_type=F32))
            b_last = b[C - 1:C, :]
            q = q_ref[0, rows, ks].astype(F32)
            k = k_ref[0, rows, ks].astype(F32)
            v = v_ref[0, rows, vs]
            q_dec = (q * jnp.exp(b)).astype(BF16)
            k_inv = (k * jnp.exp(-b)).astype(BF16)
            k_tail = (k * jnp.exp(b_last - b)).astype(BF16)
            attn = lax.dot_general(q_dec, k_inv, nt, preferred_element_type=F32)
            attn = jnp.where(causal, attn, 0.0).astype(BF16)
            state = state_ref[h]
            o = (jnp.dot(attn, v, preferred_element_type=F32)
                 + jnp.dot(q_dec, state.astype(BF16), preferred_element_type=F32))
            decay = jnp.exp(jnp.broadcast_to(b_last, (GLA_K_DIM, GLA_K_DIM)).T)
            decay = jnp.concatenate([decay, decay], axis=1)
            state_ref[h] = state * decay + lax.dot_general(
                k_tail, v, tn, preferred_element_type=F32)
            ms = jnp.mean(o * o, axis=-1, keepdims=True)
            y = o * lax.rsqrt(ms + NORM_EPS) * g_ref[...]
            o_ref[0, rows, vs] = (y * gate_ref[0, rows, vs].astype(F32)).astype(BF16)


def _gla(qb, kb, vb, la, gate, gla_g):
    B, S, _ = qb.shape
    tg = GLA_ROWS
    row3 = lambda width: pl.BlockSpec((1, tg, width), lambda b, i: (b, i, 0))
    return pl.pallas_call(
        _gla_kernel,
        grid=(B, S // tg),
        in_specs=[row3(GLA_QK_WIDTH), row3(GLA_QK_WIDTH), row3(GLA_WIDTH),
                  row3(GLA_QK_WIDTH), row3(GLA_WIDTH),
                  pl.BlockSpec((1, GLA_V_DIM), lambda b, i: (0, 0))],
        out_specs=row3(GLA_WIDTH),
        out_shape=jax.ShapeDtypeStruct((B, S, GLA_WIDTH), BF16),
        scratch_shapes=[pltpu.VMEM((GLA_HEADS, GLA_K_DIM, GLA_V_DIM), F32)],
        compiler_params=pltpu.CompilerParams(
            dimension_semantics=("parallel", "arbitrary"),
            vmem_limit_bytes=VMEM_LIMIT),
        name="gla",
    )(qb, kb, vb, la, gate, gla_g.reshape(1, GLA_V_DIM))


def _out_proj_kernel(oa_ref, ob_ref, w_ref, x_ref, g_ref, o_ref):
    y = (jnp.dot(oa_ref[0], w_ref[:DA_WIDTH, :], preferred_element_type=F32)
         + jnp.dot(ob_ref[0], w_ref[DA_WIDTH:, :], preferred_element_type=F32))
    ms = jnp.mean(y * y, axis=-1, keepdims=True)
    o_ref[0] = x_ref[0] + y * lax.rsqrt(ms + NORM_EPS) * g_ref[...]


def _out_proj(oa, ob, w_out, x, post_g):
    B, S, D = x.shape
    tm = OUT_ROWS
    row3 = lambda width: pl.BlockSpec((1, tm, width), lambda b, i: (b, i, 0))
    const = dict(pipeline_mode=pl.Buffered(1))
    return pl.pallas_call(
        _out_proj_kernel,
        grid=(B, S // tm),
        in_specs=[row3(DA_WIDTH), row3(GLA_WIDTH),
                  pl.BlockSpec(w_out.shape, lambda b, i: (0, 0), **const),
                  row3(D),
                  pl.BlockSpec((1, D), lambda b, i: (0, 0), **const)],
        out_specs=row3(D),
        out_shape=jax.ShapeDtypeStruct((B, S, D), F32),
        compiler_params=pltpu.CompilerParams(
            dimension_semantics=("parallel", "parallel"),
            vmem_limit_bytes=VMEM_LIMIT),
        name="out_proj",
    )(oa, ob, w_out.astype(BF16), x, post_g.reshape(1, D))


def kernel(x, pre_norm_g, post_norm_g, w_in, w_gk_up, b_gk, lambda_q1, lambda_k1,
           lambda_q2, lambda_k2, attn_subln_g, gla_norm_g, w_out):
    assert x.shape[-1] == D_MODEL and pre_norm_g.shape[0] == 1
    qt, vt, k, ga, qb, kb, vb, gb, la = _in_proj(
        x, pre_norm_g[0], w_in[0], w_gk_up[0], b_gk[0])
    oa = _diff_attn(qt, k, vt, ga, lambda_q1[0], lambda_k1[0], lambda_q2[0],
                    lambda_k2[0], attn_subln_g[0])
    ob = _gla(qb, kb, vb, la, gb, gla_norm_g[0])
    return _out_proj(oa, ob, w_out[0], x, post_norm_g[0])
```

```python
import math

import jax
import jax.numpy as jnp
from jax import lax
from jax.experimental import pallas as pl
from jax.experimental.pallas import tpu as pltpu

D_MODEL = 1024
DA_HEADS = 8
DA_HEAD_DIM = 64
DA_V_DIM = 128
DA_WIDTH = DA_HEADS * DA_V_DIM
GLA_HEADS = 4
GLA_K_DIM = 128
GLA_V_DIM = 256
GLA_QK_WIDTH = GLA_HEADS * GLA_K_DIM
GLA_WIDTH = GLA_HEADS * GLA_V_DIM
GLA_GATE_RANK = 16
GLA_GATE_NORMALIZER = 16.0
GLA_CHUNK = 64
NORM_EPS = 1e-6
LAM_INIT = 0.8 - 0.6 * math.exp(-0.3 * 0)

LANE = 128
BF16_SUBLANES = 16
VMEM_LIMIT = 56 * 1024 * 1024

PROJ_ROWS = 256
ATT_TQ = 512
ATT_TK = PROJ_ROWS
ATT_VROWS = DA_V_DIM + BF16_SUBLANES
GLA_ROWS = 256
OUT_ROWS = 512
NEG_BIG = -1e30

BF16 = jnp.bfloat16
F32 = jnp.float32


def _silu(x):
    return x * (1.0 / (1.0 + jnp.exp(-x)))


def _log_sigmoid(x):
    return jnp.minimum(x, 0.0) - jnp.log(1.0 + jnp.exp(-jnp.abs(x)))


def _in_proj_kernel(x_ref, g_ref, wt_ref, w_ref, wgk_ref, bgk_ref,
                    qt_ref, vt_ref, k_ref, ga_ref, qb_ref, kb_ref, vb_ref,
                    gb_ref, la_ref):
    x = x_ref[0]
    ms = jnp.mean(x * x, axis=-1, keepdims=True)
    h = (x * lax.rsqrt(ms + NORM_EPS) * g_ref[...]).astype(BF16)

    nt = (((1,), (1,)), ((), ()))
    zt = lax.dot_general(wt_ref[...], h, nt, preferred_element_type=F32)
    q_scale = DA_HEAD_DIM ** -0.5 * math.log2(math.e)
    ones = jnp.ones((BF16_SUBLANES, x.shape[0]), BF16)
    for hh in range(DA_HEADS):
        lo = hh * LANE
        qt_ref[0, hh] = (zt[lo:lo + LANE] * q_scale).astype(BF16)
        vt_ref[0, hh, 0, :DA_V_DIM, :] = zt[DA_WIDTH + lo:DA_WIDTH + lo + LANE].astype(BF16)
        vt_ref[0, hh, 0, DA_V_DIM:, :] = ones

    def proj(c0, width):
        return jnp.dot(h, w_ref[:, c0:c0 + width], preferred_element_type=F32)

    zk = proj(0, 1024)
    for hh in range(DA_HEADS):
        k_ref[0, hh] = zk[:, hh * LANE:(hh + 1) * LANE].astype(BF16)
    ga_ref[0] = _silu(proj(1024, 1024)).astype(BF16)
    zqk = proj(2048, 1024)
    qb_ref[0] = (zqk[:, :GLA_QK_WIDTH] * (GLA_K_DIM ** -0.5)).astype(BF16)
    kb_ref[0] = zqk[:, GLA_QK_WIDTH:].astype(BF16)
    vb_ref[0] = proj(3072, 1024).astype(BF16)
    gb_ref[0] = _silu(proj(4096, 1024)).astype(BF16)
    gk_low = proj(5120, LANE).astype(BF16)
    gk = jnp.dot(gk_low, wgk_ref[...], preferred_element_type=F32) + bgk_ref[...]
    la_ref[0] = _log_sigmoid(gk) * (1.0 / GLA_GATE_NORMALIZER)


def _in_proj(x, pre_g, w_in, w_gk_up, b_gk):
    B, S, D = x.shape
    tm = PROJ_ROWS
    c = 0
    cols = {}
    for name, width in (("qa", 1024), ("ka", 1024), ("va", 1024), ("ga", 1024),
                        ("qb", 512), ("kb", 512), ("vb", 1024), ("gb", 1024),
                        ("gk", GLA_GATE_RANK)):
        cols[name] = (c, c + width)
        c += width

    def sl(name):
        a, b = cols[name]
        return w_in[:, a:b]

    wt = jnp.concatenate([sl("qa"), sl("va")], axis=1).T.astype(BF16)
    w = jnp.concatenate(
        [sl("ka"), sl("ga"), sl("qb"), sl("kb"), sl("vb"), sl("gb"),
         jnp.pad(sl("gk"), ((0, 0), (0, LANE - GLA_GATE_RANK)))], axis=1).astype(BF16)
    wgk = jnp.pad(w_gk_up, ((0, LANE - GLA_GATE_RANK), (0, 0))).astype(BF16)
    bgk = b_gk.reshape(1, GLA_QK_WIDTH)
    g = pre_g.reshape(1, D)

    const = dict(pipeline_mode=pl.Buffered(1))
    in_specs = [
        pl.BlockSpec((1, tm, D), lambda b, i: (b, i, 0)),
        pl.BlockSpec((1, D), lambda b, i: (0, 0), **const),
        pl.BlockSpec(wt.shape, lambda b, i: (0, 0), **const),
        pl.BlockSpec(w.shape, lambda b, i: (0, 0), **const),
        pl.BlockSpec(wgk.shape, lambda b, i: (0, 0), **const),
        pl.BlockSpec(bgk.shape, lambda b, i: (0, 0), **const),
    ]
    row3 = lambda width: pl.BlockSpec((1, tm, width), lambda b, i: (b, i, 0))
    out_specs = [
        pl.BlockSpec((1, DA_HEADS, LANE, tm), lambda b, i: (b, 0, 0, i)),
        pl.BlockSpec((1, DA_HEADS, 1, ATT_VROWS, tm), lambda b, i: (b, 0, i, 0, 0)),
        pl.BlockSpec((1, DA_HEADS, tm, LANE), lambda b, i: (b, 0, i, 0)),
        row3(DA_WIDTH),
        row3(GLA_QK_WIDTH), row3(GLA_QK_WIDTH), row3(GLA_WIDTH), row3(GLA_WIDTH),
        row3(GLA_QK_WIDTH),
    ]
    out_shape = [
        jax.ShapeDtypeStruct((B, DA_HEADS, LANE, S), BF16),
        jax.ShapeDtypeStruct((B, DA_HEADS, S // tm, ATT_VROWS, tm), BF16),
        jax.ShapeDtypeStruct((B, DA_HEADS, S, LANE), BF16),
        jax.ShapeDtypeStruct((B, S, DA_WIDTH), BF16),
        jax.ShapeDtypeStruct((B, S, GLA_QK_WIDTH), BF16),
        jax.ShapeDtypeStruct((B, S, GLA_QK_WIDTH), BF16),
        jax.ShapeDtypeStruct((B, S, GLA_WIDTH), BF16),
        jax.ShapeDtypeStruct((B, S, GLA_WIDTH), BF16),
        jax.ShapeDtypeStruct((B, S, GLA_QK_WIDTH), F32),
    ]
    return pl.pallas_call(
        _in_proj_kernel,
        grid=(B, S // tm),
        in_specs=in_specs,
        out_specs=out_specs,
        out_shape=out_shape,
        compiler_params=pltpu.CompilerParams(
            dimension_semantics=("parallel", "parallel"),
            vmem_limit_bytes=VMEM_LIMIT),
        name="in_proj",
    )(x, g, wt, w, wgk, bgk)


def _diff_attn_kernel(qt_ref, k_ref, vt_ref, gate_ref, lq1_ref, lk1_ref, lq2_ref,
                      lk2_ref, g_ref, o_ref, acc1, acc2, m1, m2,
                      sa1, sa2, sb1, sb2, ca1, ca2, cb1, cb2):
    tq, tk = ATT_TQ, ATT_TK
    qi = pl.program_id(2)
    qt = qt_ref[0, 0]
    row = lax.broadcasted_iota(jnp.int32, qt.shape, 0)
    zero = jnp.zeros_like(qt)
    q_maps = (jnp.where(row < DA_HEAD_DIM, qt, zero),
              jnp.where(row >= DA_HEAD_DIM, qt, zero))
    accs, ms = (acc1, acc2), (m1, m2)
    slots = (((sa1, sa2), (ca1, ca2)), ((sb1, sb2), (cb1, cb2)))
    for acc, m in zip(accs, ms):
        acc[...] = jnp.zeros_like(acc)
        m[...] = jnp.full_like(m, NEG_BIG)

    def scores(j):
        off = pl.multiple_of(j * tk, tk)
        k = k_ref[0, 0, pl.ds(off, tk), :]
        return [jnp.dot(k, qm, preferred_element_type=F32) for qm in q_maps]

    def park(s_list, slot):
        s_refs, c_refs = slots[slot]
        for s, s_ref, c_ref in zip(s_list, s_refs, c_refs):
            s_ref[...] = s
            c_ref[...] = jnp.max(s, axis=0, keepdims=True)

    def consume(j, s_list, c_list):
        vt = vt_ref[0, 0, j]
        out = []
        for s, c, m in zip(s_list, c_list, ms):
            m_old = m[...]
            m_new = jnp.maximum(m_old, c)
            alpha = jnp.exp2(m_old - m_new)
            p = jnp.exp2(s - m_new).astype(BF16)
            out.append((alpha, m_new, jnp.dot(vt, p, preferred_element_type=F32)))
        return out

    def finish(out):
        for (alpha, m_new, pv), acc, m in zip(out, accs, ms):
            acc[...] = alpha * acc[...] + pv
            m[...] = m_new

    sub = tq // tk
    n_full = qi * sub

    key_rel = lax.broadcasted_iota(jnp.int32, (tk, tq), 0)
    qry_rel = lax.broadcasted_iota(jnp.int32, (tk, tq), 1)
    diag = [scores(n_full + jj) for jj in range(sub)]
    park(scores(0), 0)
    for jj in range(sub):
        mask = key_rel + jj * tk <= qry_rel
        s_list = [jnp.where(mask, s, NEG_BIG) for s in diag[jj]]
        c_list = [jnp.max(s, axis=0, keepdims=True) for s in s_list]
        finish(consume(n_full + jj, s_list, c_list))

    def pipelined(j, slot):
        nxt = scores(j + 1)
        s_refs, c_refs = slots[slot]
        out = consume(j, [r[...] for r in s_refs], [r[...] for r in c_refs])
        park(nxt, 1 - slot)
        finish(out)

    def body(i, carry):
        pipelined(2 * i, 0)
        pipelined(2 * i + 1, 1)
        return carry

    lax.fori_loop(0, n_full // 2, body, 0)

    lam = (jnp.exp(jnp.sum(lq1_ref[...] * lk1_ref[...], axis=-1, keepdims=True))
           - jnp.exp(jnp.sum(lq2_ref[...] * lk2_ref[...], axis=-1, keepdims=True))
           + LAM_INIT)
    a1, a2 = acc1[...], acc2[...]
    l1, l2 = a1[DA_V_DIM:DA_V_DIM + 1], a2[DA_V_DIM:DA_V_DIM + 1]
    o = a1[:DA_V_DIM] * (1.0 / l1) - a2[:DA_V_DIM] * (lam / l2)
    msq = jnp.mean(o * o, axis=0, keepdims=True)
    y = o * lax.rsqrt(msq + NORM_EPS) * g_ref[...] * (1.0 - LAM_INIT)
    o_ref[0] = (y.T * gate_ref[0].astype(F32)).astype(BF16)


def _diff_attn(qt, k, vt, gate, lq1, lk1, lq2, lk2, subln_g):
    B, H, _, S = qt.shape
    tq, tk = ATT_TQ, ATT_TK
    assert tq == 2 * tk
    lam_spec = pl.BlockSpec((1, DA_HEAD_DIM), lambda b, h, i: (0, 0))
    score_buf = pltpu.VMEM((tk, tq), F32)
    cmax_buf = pltpu.VMEM((1, tq), F32)
    return pl.pallas_call(
        _diff_attn_kernel,
        grid=(B, H, S // tq),
        in_specs=[
            pl.BlockSpec((1, 1, LANE, tq), lambda b, h, i: (b, h, 0, i)),
            pl.BlockSpec((1, 1, S, LANE), lambda b, h, i: (b, h, 0, 0)),
            pl.BlockSpec((1, 1, S // tk, ATT_VROWS, tk), lambda b, h, i: (b, h, 0, 0, 0)),
            pl.BlockSpec((1, tq, LANE), lambda b, h, i: (b, i, h)),
            lam_spec, lam_spec, lam_spec, lam_spec,
            pl.BlockSpec((DA_V_DIM, 1), lambda b, h, i: (0, 0)),
        ],
        out_specs=pl.BlockSpec((1, tq, LANE), lambda b, h, i: (b, i, h)),
        out_shape=jax.ShapeDtypeStruct((B, S, DA_WIDTH), BF16),
        scratch_shapes=[
            pltpu.VMEM((ATT_VROWS, tq), F32), pltpu.VMEM((ATT_VROWS, tq), F32),
            cmax_buf, cmax_buf,
            score_buf, score_buf, score_buf, score_buf,
            cmax_buf, cmax_buf, cmax_buf, cmax_buf,
        ],
        compiler_params=pltpu.CompilerParams(
            dimension_semantics=("parallel", "parallel", "arbitrary"),
            vmem_limit_bytes=VMEM_LIMIT),
        name="diff_attn",
    )(qt, k, vt, gate, lq1.reshape(1, -1), lk1.reshape(1, -1), lq2.reshape(1, -1),
      lk2.reshape(1, -1), subln_g.reshape(DA_V_DIM, 1))


def _gla_kernel(q_ref, k_ref, v_ref, la_ref, gate_ref, g_ref, o_ref, state_ref):
    C = GLA_CHUNK

    @pl.when(pl.program_id(1) == 0)
    def _():
        state_ref[...] = jnp.zeros_like(state_ref)

    r = lax.broadcasted_iota(jnp.int32, (C, C), 0)
    c = lax.broadcasted_iota(jnp.int32, (C, C), 1)
    causal = c <= r
    tri = jnp.where(causal, 1.0, 0.0).astype(BF16)
    tn = (((0,), (0,)), ((), ()))
    nt = (((1,), (1,)), ((), ()))
    for ci in range(GLA_ROWS // C):
        rows = slice(ci * C, (ci + 1) * C)
        for h in range(GLA_HEADS):
            ks = slice(h * GLA_K_DIM, (h + 1) * GLA_K_DIM)
            vs = slice(h * GLA_V_DIM, (h + 1) * GLA_V_DIM)
            la = la_ref[0, rows, ks]
            la_hi = la.astype(BF16)
            la_lo = (la - la_hi.astype(F32)).astype(BF16)
            b = (jnp.dot(tri, la_hi, preferred_element_type=F32)
                 + jnp.dot(tri, la_lo, preferred_element_type=F32))
            b_last = b[C - 1:C, :]
            q = q_ref[0, rows, ks].astype(F32)
            k = k_ref[0, rows, ks].astype(F32)
            v = v_ref[0, rows, vs]
            q_dec = (q * jnp.exp(b)).astype(BF16)
            k_inv = (k * jnp.exp(-b)).astype(BF16)
            k_tail = (k * jnp.exp(b_last - b)).astype(BF16)
            attn = lax.dot_general(q_dec, k_inv, nt, preferred_element_type=F32)
            attn = jnp.where(causal, attn, 0.0).astype(BF16)
            state = state_ref[h]
            o = (jnp.dot(attn, v, preferred_element_type=F32)
                 + jnp.dot(q_dec, state.astype(BF16), preferred_element_type=F32))
            decay = jnp.exp(jnp.broadcast_to(b_last, (GLA_K_DIM, GLA_K_DIM)).T)
            decay = jnp.concatenate([decay, decay], axis=1)
            state_ref[h] = state * decay + lax.dot_general(
                k_tail, v, tn, preferred_element_type=F32)
            ms = jnp.mean(o * o, axis=-1, keepdims=True)
            y = o * lax.rsqrt(ms + NORM_EPS) * g_ref[...]
            o_ref[0, rows, vs] = (y * gate_ref[0, rows, vs].astype(F32)).astype(BF16)


def _gla(qb, kb, vb, la, gate, gla_g):
    B, S, _ = qb.shape
    tg = GLA_ROWS
    row3 = lambda width: pl.BlockSpec((1, tg, width), lambda b, i: (b, i, 0))
    return pl.pallas_call(
        _gla_kernel,
        grid=(B, S // tg),
        in_specs=[row3(GLA_QK_WIDTH), row3(GLA_QK_WIDTH), row3(GLA_WIDTH),
                  row3(GLA_QK_WIDTH), row3(GLA_WIDTH),
                  pl.BlockSpec((1, GLA_V_DIM), lambda b, i: (0, 0))],
        out_specs=row3(GLA_WIDTH),
        out_shape=jax.ShapeDtypeStruct((B, S, GLA_WIDTH), BF16),
        scratch_shapes=[pltpu.VMEM((GLA_HEADS, GLA_K_DIM, GLA_V_DIM), F32)],
        compiler_params=pltpu.CompilerParams(
            dimension_semantics=("parallel", "arbitrary"),
            vmem_limit_bytes=VMEM_LIMIT),
        name="gla",
    )(qb, kb, vb, la, gate, gla_g.reshape(1, GLA_V_DIM))


def _out_proj_kernel(oa_ref, ob_ref, w_ref, x_ref, g_ref, o_ref):
    y = (jnp.dot(oa_ref[0], w_ref[:DA_WIDTH, :], preferred_element_type=F32)
         + jnp.dot(ob_ref[0], w_ref[DA_WIDTH:, :], preferred_element_type=F32))
    ms = jnp.mean(y * y, axis=-1, keepdims=True)
    o_ref[0] = x_ref[0] + y * lax.rsqrt(ms + NORM_EPS) * g_ref[...]


def _out_proj(oa, ob, w_out, x, post_g):
    B, S, D = x.shape
    tm = OUT_ROWS
    row3 = lambda width: pl.BlockSpec((1, tm, width), lambda b, i: (b, i, 0))
    const = dict(pipeline_mode=pl.Buffered(1))
    return pl.pallas_call(
        _out_proj_kernel,
        grid=(B, S // tm),
        in_specs=[row3(DA_WIDTH), row3(GLA_WIDTH),
                  pl.BlockSpec(w_out.shape, lambda b, i: (0, 0), **const),
                  row3(D),
                  pl.BlockSpec((1, D), lambda b, i: (0, 0), **const)],
        out_specs=row3(D),
        out_shape=jax.ShapeDtypeStruct((B, S, D), F32),
        compiler_params=pltpu.CompilerParams(
            dimension_semantics=("parallel", "parallel"),
            vmem_limit_bytes=VMEM_LIMIT),
        name="out_proj",
    )(oa, ob, w_out.astype(BF16), x, post_g.reshape(1, D))


def kernel(x, pre_norm_g, post_norm_g, w_in, w_gk_up, b_gk, lambda_q1, lambda_k1,
           lambda_q2, lambda_k2, attn_subln_g, gla_norm_g, w_out):
    assert x.shape[-1] == D_MODEL and pre_norm_g.shape[0] == 1
    qt, vt, k, ga, qb, kb, vb, gb, la = _in_proj(
        x, pre_norm_g[0], w_in[0], w_gk_up[0], b_gk[0])
    oa = _diff_attn(qt, k, vt, ga, lambda_q1[0], lambda_k1[0], lambda_q2[0],
                    lambda_k2[0], attn_subln_g[0])
    ob = _gla(qb, kb, vb, la, gb, gla_norm_g[0])
    return _out_proj(oa, ob, w_out[0], x, post_norm_g[0])
```

```python
import math

import jax
import jax.numpy as jnp
from jax import lax
from jax.experimental import pallas as pl
from jax.experimental.pallas import tpu as pltpu

D_MODEL = 1024
DA_HEADS = 8
DA_HEAD_DIM = 64
DA_V_DIM = 128
DA_WIDTH = DA_HEADS * DA_V_DIM
GLA_HEADS = 4
GLA_K_DIM = 128
GLA_V_DIM = 256
GLA_QK_WIDTH = GLA_HEADS * GLA_K_DIM
GLA_WIDTH = GLA_HEADS * GLA_V_DIM
GLA_GATE_RANK = 16
GLA_GATE_NORMALIZER = 16.0
GLA_CHUNK = 64
NORM_EPS = 1e-6
LAM_INIT = 0.8 - 0.6 * math.exp(-0.3 * 0)

LANE = 128
BF16_SUBLANES = 16
VMEM_LIMIT = 56 * 1024 * 1024

PROJ_ROWS = 256
ATT_TQ = 1024
ATT_TK = PROJ_ROWS
ATT_VROWS = DA_V_DIM + BF16_SUBLANES
ATT_TN = 256
GLA_ROWS = 256
OUT_ROWS = 512
NEG_BIG = -1e30

BF16 = jnp.bfloat16
F32 = jnp.float32


def _silu(x):
    return x * (1.0 / (1.0 + jnp.exp(-x)))


def _log_sigmoid(x):
    return jnp.minimum(x, 0.0) - jnp.log(1.0 + jnp.exp(-jnp.abs(x)))


def _in_proj_kernel(x_ref, g_ref, wt_ref, w_ref, wgk_ref, bgk_ref,
                    qt_ref, vt_ref, k_ref, ga_ref, qb_ref, kb_ref, vb_ref,
                    gb_ref, la_ref):
    x = x_ref[0]
    ms = jnp.mean(x * x, axis=-1, keepdims=True)
    h = (x * lax.rsqrt(ms + NORM_EPS) * g_ref[...]).astype(BF16)

    nt = (((1,), (1,)), ((), ()))
    zt = lax.dot_general(wt_ref[...], h, nt, preferred_element_type=F32)
    q_scale = DA_HEAD_DIM ** -0.5 * math.log2(math.e)
    ones = jnp.ones((BF16_SUBLANES, x.shape[0]), BF16)
    for hh in range(DA_HEADS):
        lo = hh * LANE
        qt_ref[0, hh] = (zt[lo:lo + LANE] * q_scale).astype(BF16)
        vt_ref[0, hh, 0, :DA_V_DIM, :] = zt[DA_WIDTH + lo:DA_WIDTH + lo + LANE].astype(BF16)
        vt_ref[0, hh, 0, DA_V_DIM:, :] = ones

    def proj(c0, width):
        return jnp.dot(h, w_ref[:, c0:c0 + width], preferred_element_type=F32)

    zk = proj(0, 1024)
    for hh in range(DA_HEADS):
        k_ref[0, hh] = zk[:, hh * LANE:(hh + 1) * LANE].astype(BF16)
    ga_ref[0] = _silu(proj(1024, 1024)).astype(BF16)
    zqk = proj(2048, 1024)
    qb_ref[0] = (zqk[:, :GLA_QK_WIDTH] * (GLA_K_DIM ** -0.5)).astype(BF16)
    kb_ref[0] = zqk[:, GLA_QK_WIDTH:].astype(BF16)
    vb_ref[0] = proj(3072, 1024).astype(BF16)
    gb_ref[0] = _silu(proj(4096, 1024)).astype(BF16)
    gk_low = proj(5120, LANE).astype(BF16)
    gk = jnp.dot(gk_low, wgk_ref[...], preferred_element_type=F32) + bgk_ref[...]
    la_ref[0] = _log_sigmoid(gk) * (1.0 / GLA_GATE_NORMALIZER)


def _in_proj(x, pre_g, w_in, w_gk_up, b_gk):
    B, S, D = x.shape
    tm = PROJ_ROWS
    c = 0
    cols = {}
    for name, width in (("qa", 1024), ("ka", 1024), ("va", 1024), ("ga", 1024),
                        ("qb", 512), ("kb", 512), ("vb", 1024), ("gb", 1024),
                        ("gk", GLA_GATE_RANK)):
        cols[name] = (c, c + width)
        c += width

    def sl(name):
        a, b = cols[name]
        return w_in[:, a:b]

    wt = jnp.concatenate([sl("qa"), sl("va")], axis=1).T.astype(BF16)
    w = jnp.concatenate(
        [sl("ka"), sl("ga"), sl("qb"), sl("kb"), sl("vb"), sl("gb"),
         jnp.pad(sl("gk"), ((0, 0), (0, LANE - GLA_GATE_RANK)))], axis=1).astype(BF16)
    wgk = jnp.pad(w_gk_up, ((0, LANE - GLA_GATE_RANK), (0, 0))).astype(BF16)
    bgk = b_gk.reshape(1, GLA_QK_WIDTH)
    g = pre_g.reshape(1, D)

    const = dict(pipeline_mode=pl.Buffered(1))
    in_specs = [
        pl.BlockSpec((1, tm, D), lambda b, i: (b, i, 0)),
        pl.BlockSpec((1, D), lambda b, i: (0, 0), **const),
        pl.BlockSpec(wt.shape, lambda b, i: (0, 0), **const),
        pl.BlockSpec(w.shape, lambda b, i: (0, 0), **const),
        pl.BlockSpec(wgk.shape, lambda b, i: (0, 0), **const),
        pl.BlockSpec(bgk.shape, lambda b, i: (0, 0), **const),
    ]
    row3 = lambda width: pl.BlockSpec((1, tm, width), lambda b, i: (b, i, 0))
    out_specs = [
        pl.BlockSpec((1, DA_HEADS, LANE, tm), lambda b, i: (b, 0, 0, i)),
        pl.BlockSpec((1, DA_HEADS, 1, ATT_VROWS, tm), lambda b, i: (b, 0, i, 0, 0)),
        pl.BlockSpec((1, DA_HEADS, tm, LANE), lambda b, i: (b, 0, i, 0)),
        row3(DA_WIDTH),
        row3(GLA_QK_WIDTH), row3(GLA_QK_WIDTH), row3(GLA_WIDTH), row3(GLA_WIDTH),
        row3(GLA_QK_WIDTH),
    ]
    out_shape = [
        jax.ShapeDtypeStruct((B, DA_HEADS, LANE, S), BF16),
        jax.ShapeDtypeStruct((B, DA_HEADS, S // tm, ATT_VROWS, tm), BF16),
        jax.ShapeDtypeStruct((B, DA_HEADS, S, LANE), BF16),
        jax.ShapeDtypeStruct((B, S, DA_WIDTH), BF16),
        jax.ShapeDtypeStruct((B, S, GLA_QK_WIDTH), BF16),
        jax.ShapeDtypeStruct((B, S, GLA_QK_WIDTH), BF16),
        jax.ShapeDtypeStruct((B, S, GLA_WIDTH), BF16),
        jax.ShapeDtypeStruct((B, S, GLA_WIDTH), BF16),
        jax.ShapeDtypeStruct((B, S, GLA_QK_WIDTH), F32),
    ]
    return pl.pallas_call(
        _in_proj_kernel,
        grid=(B, S // tm),
        in_specs=in_specs,
        out_specs=out_specs,
        out_shape=out_shape,
        compiler_params=pltpu.CompilerParams(
            dimension_semantics=("parallel", "parallel"),
            vmem_limit_bytes=VMEM_LIMIT),
        name="in_proj",
    )(x, g, wt, w, wgk, bgk)


def _diff_attn_kernel(qt_ref, k_ref, vt_ref, gate_ref, lq1_ref, lk1_ref, lq2_ref,
                      lk2_ref, g_ref, o_ref, acc_ref, m_ref, s_ref, c_ref):
    tq, tk, tn = ATT_TQ, ATT_TK, ATT_TN
    n_tiles = tq // tn
    qi = pl.program_id(2)
    qt = qt_ref[0, 0]
    row = lax.broadcasted_iota(jnp.int32, qt.shape, 0)
    zero = jnp.zeros_like(qt)
    q_maps = (jnp.where(row < DA_HEAD_DIM, qt, zero),
              jnp.where(row >= DA_HEAD_DIM, qt, zero))
    groups = [(mp, n) for mp in range(2) for n in range(n_tiles)]
    qw = [q_maps[mp][:, n * tn:(n + 1) * tn] for mp, n in groups]
    acc_ref[...] = jnp.zeros_like(acc_ref)
    m_ref[...] = jnp.full_like(m_ref, NEG_BIG)
    causal = (lax.broadcasted_iota(jnp.int32, (tk, tn), 0)
              <= lax.broadcasted_iota(jnp.int32, (tk, tn), 1))

    def key_tile(j):
        return k_ref[0, 0, pl.ds(pl.multiple_of(j * tk, tk), tk), :]

    def issue(k, g, slot, masked):
        s = jnp.dot(k, qw[g], preferred_element_type=F32)
        if masked:
            s = jnp.where(causal, s, NEG_BIG)
        s_ref[slot, g] = s
        c_ref[slot, g] = jnp.max(s, axis=0, keepdims=True)

    def consume(vt, g, slot):
        m_old = m_ref[g]
        m_new = jnp.maximum(m_old, c_ref[slot, g])
        alpha = jnp.exp2(m_old - m_new)
        p = jnp.exp2(s_ref[slot, g] - m_new).astype(BF16)
        acc_ref[g] = alpha * acc_ref[g] + jnp.dot(vt, p, preferred_element_type=F32)
        m_ref[g] = m_new

    def step(k_next, nxt, vt, cur, slot):
        for idx in range(max(len(nxt), len(cur))):
            if idx < len(nxt):
                issue(k_next, nxt[idx][0], 1 - slot, nxt[idx][1])
            if idx < len(cur):
                consume(vt, cur[idx], slot)

    sub = tq // tk
    n_full = qi * sub

    def diag_groups(jj):
        return [(g, n == jj) for g, (_, n) in enumerate(groups) if n >= jj]

    for g, masked in diag_groups(0):
        issue(key_tile(n_full), g, 0, masked)
    for jj in range(sub):
        if jj + 1 < sub:
            k_next, nxt = key_tile(n_full + jj + 1), diag_groups(jj + 1)
        else:
            k_next, nxt = key_tile(0), [(g, False) for g in range(len(groups))]
        step(k_next, nxt, vt_ref[0, 0, n_full + jj], [g for g, _ in diag_groups(jj)], jj % 2)

    all_groups = list(range(len(groups)))

    def full_step(j, slot):
        step(key_tile(j + 1), [(g, False) for g in all_groups], vt_ref[0, 0, j], all_groups, slot)

    def body(i, carry):
        full_step(2 * i, 0)
        full_step(2 * i + 1, 1)
        return carry

    lax.fori_loop(0, n_full // 2, body, 0)

    lam = (jnp.exp(jnp.sum(lq1_ref[...] * lk1_ref[...], axis=-1, keepdims=True))
           - jnp.exp(jnp.sum(lq2_ref[...] * lk2_ref[...], axis=-1, keepdims=True))
           + LAM_INIT)
    for n in range(n_tiles):
        a1, a2 = acc_ref[n], acc_ref[n_tiles + n]
        l1, l2 = a1[DA_V_DIM:DA_V_DIM + 1], a2[DA_V_DIM:DA_V_DIM + 1]
        o = a1[:DA_V_DIM] * (1.0 / l1) - a2[:DA_V_DIM] * (lam / l2)
        msq = jnp.mean(o * o, axis=0, keepdims=True)
        y = o * lax.rsqrt(msq + NORM_EPS) * g_ref[...] * (1.0 - LAM_INIT)
        rows = slice(n * tn, (n + 1) * tn)
        o_ref[0, rows, :] = (y.T * gate_ref[0, rows, :].astype(F32)).astype(BF16)


def _diff_attn(qt, k, vt, gate, lq1, lk1, lq2, lk2, subln_g):
    B, H, _, S = qt.shape
    tq, tk, tn = ATT_TQ, ATT_TK, ATT_TN
    assert tk == tn and (tq // tk) % 2 == 0
    n_col_groups = 2 * (tq // tn)
    lam_spec = pl.BlockSpec((1, DA_HEAD_DIM), lambda b, h, i: (0, 0))
    return pl.pallas_call(
        _diff_attn_kernel,
        grid=(B, H, S // tq),
        in_specs=[
            pl.BlockSpec((1, 1, LANE, tq), lambda b, h, i: (b, h, 0, i)),
            pl.BlockSpec((1, 1, S, LANE), lambda b, h, i: (b, h, 0, 0)),
            pl.BlockSpec((1, 1, S // tk, ATT_VROWS, tk), lambda b, h, i: (b, h, 0, 0, 0)),
            pl.BlockSpec((1, tq, LANE), lambda b, h, i: (b, i, h)),
            lam_spec, lam_spec, lam_spec, lam_spec,
            pl.BlockSpec((DA_V_DIM, 1), lambda b, h, i: (0, 0)),
        ],
        out_specs=pl.BlockSpec((1, tq, LANE), lambda b, h, i: (b, i, h)),
        out_shape=jax.ShapeDtypeStruct((B, S, DA_WIDTH), BF16),
        scratch_shapes=[
            pltpu.VMEM((n_col_groups, ATT_VROWS, tn), F32),
            pltpu.VMEM((n_col_groups, 1, tn), F32),
            pltpu.VMEM((2, n_col_groups, tk, tn), F32),
            pltpu.VMEM((2, n_col_groups, 1, tn), F32),
        ],
        compiler_params=pltpu.CompilerParams(
            dimension_semantics=("parallel", "parallel", "arbitrary"),
            vmem_limit_bytes=VMEM_LIMIT),
        name="diff_attn",
    )(qt, k, vt, gate, lq1.reshape(1, -1), lk1.reshape(1, -1), lq2.reshape(1, -1),
      lk2.reshape(1, -1), subln_g.reshape(DA_V_DIM, 1))


def _gla_kernel(q_ref, k_ref, v_ref, la_ref, gate_ref, g_ref, o_ref, state_ref):
    C = GLA_CHUNK
    heads = range(GLA_HEADS)

    @pl.when(pl.program_id(1) == 0)
    def _():
        state_ref[...] = jnp.zeros_like(state_ref)

    r = lax.broadcasted_iota(jnp.int32, (C, C), 0)
    c = lax.broadcasted_iota(jnp.int32, (C, C), 1)
    causal = c <= r
    tri = jnp.where(causal, 1.0, 0.0).astype(BF16)
    tn = (((0,), (0,)), ((), ()))
    nt = (((1,), (1,)), ((), ()))
    for ci in range(GLA_ROWS // C):
        rows = slice(ci * C, (ci + 1) * C)
        ks = [slice(h * GLA_K_DIM, (h + 1) * GLA_K_DIM) for h in heads]
        vs = [slice(h * GLA_V_DIM, (h + 1) * GLA_V_DIM) for h in heads]
        b = []
        for h in heads:
            la = la_ref[0, rows, ks[h]]
            la_hi = la.astype(BF16)
            la_lo = (la - la_hi.astype(F32)).astype(BF16)
            b.append(jnp.dot(tri, la_hi, preferred_element_type=F32)
                     + jnp.dot(tri, la_lo, preferred_element_type=F32))
        v = [v_ref[0, rows, vs[h]] for h in heads]
        q_dec, k_inv, k_tail, decay = [], [], [], []
        for h in heads:
            b_last = b[h][C - 1:C, :]
            q = q_ref[0, rows, ks[h]].astype(F32)
            k = k_ref[0, rows, ks[h]].astype(F32)
            q_dec.append((q * jnp.exp(b[h])).astype(BF16))
            k_inv.append((k * jnp.exp(-b[h])).astype(BF16))
            k_tail.append((k * jnp.exp(b_last - b[h])).astype(BF16))
            d = jnp.exp(jnp.broadcast_to(b_last, (GLA_K_DIM, GLA_K_DIM)).T)
            decay.append(jnp.concatenate([d, d], axis=1))
        attn = [jnp.where(causal, lax.dot_general(q_dec[h], k_inv[h], nt,
                                                  preferred_element_type=F32), 0.0).astype(BF16)
                for h in heads]
        kv = [lax.dot_general(k_tail[h], v[h], tn, preferred_element_type=F32)
              for h in heads]
        o = []
        for h in heads:
            state = state_ref[h]
            o.append(jnp.dot(attn[h], v[h], preferred_element_type=F32)
                     + jnp.dot(q_dec[h], state.astype(BF16), preferred_element_type=F32))
            state_ref[h] = state * decay[h] + kv[h]
        for h in heads:
            ms = jnp.mean(o[h] * o[h], axis=-1, keepdims=True)
            y = o[h] * lax.rsqrt(ms + NORM_EPS) * g_ref[...]
            o_ref[0, rows, vs[h]] = (y * gate_ref[0, rows, vs[h]].astype(F32)).astype(BF16)


def _gla(qb, kb, vb, la, gate, gla_g):
    B, S, _ = qb.shape
    tg = GLA_ROWS
    row3 = lambda width: pl.BlockSpec((1, tg, width), lambda b, i: (b, i, 0))
    return pl.pallas_call(
        _gla_kernel,
        grid=(B, S // tg),
        in_specs=[row3(GLA_QK_WIDTH), row3(GLA_QK_WIDTH), row3(GLA_WIDTH),
                  row3(GLA_QK_WIDTH), row3(GLA_WIDTH),
                  pl.BlockSpec((1, GLA_V_DIM), lambda b, i: (0, 0))],
        out_specs=row3(GLA_WIDTH),
        out_shape=jax.ShapeDtypeStruct((B, S, GLA_WIDTH), BF16),
        scratch_shapes=[pltpu.VMEM((GLA_HEADS, GLA_K_DIM, GLA_V_DIM), F32)],
        compiler_params=pltpu.CompilerParams(
            dimension_semantics=("parallel", "arbitrary"),
            vmem_limit_bytes=VMEM_LIMIT),
        name="gla",
    )(qb, kb, vb, la, gate, gla_g.reshape(1, GLA_V_DIM))


def _out_proj_kernel(oa_ref, ob_ref, w_ref, x_ref, g_ref, o_ref):
    y = (jnp.dot(oa_ref[0], w_ref[:DA_WIDTH, :], preferred_element_type=F32)
         + jnp.dot(ob_ref[0], w_ref[DA_WIDTH:, :], preferred_element_type=F32))
    ms = jnp.mean(y * y, axis=-1, keepdims=True)
    o_ref[0] = x_ref[0] + y * lax.rsqrt(ms + NORM_EPS) * g_ref[...]


def _out_proj(oa, ob, w_out, x, post_g):
    B, S, D = x.shape
    tm = OUT_ROWS
    row3 = lambda width: pl.BlockSpec((1, tm, width), lambda b, i: (b, i, 0))
    const = dict(pipeline_mode=pl.Buffered(1))
    return pl.pallas_call(
        _out_proj_kernel,
        grid=(B, S // tm),
        in_specs=[row3(DA_WIDTH), row3(GLA_WIDTH),
                  pl.BlockSpec(w_out.shape, lambda b, i: (0, 0), **const),
                  row3(D),
                  pl.BlockSpec((1, D), lambda b, i: (0, 0), **const)],
        out_specs=row3(D),
        out_shape=jax.ShapeDtypeStruct((B, S, D), F32),
        compiler_params=pltpu.CompilerParams(
            dimension_semantics=("parallel", "parallel"),
            vmem_limit_bytes=VMEM_LIMIT),
        name="out_proj",
    )(oa, ob, w_out.astype(BF16), x, post_g.reshape(1, D))


def kernel(x, pre_norm_g, post_norm_g, w_in, w_gk_up, b_gk, lambda_q1, lambda_k1,
           lambda_q2, lambda_k2, attn_subln_g, gla_norm_g, w_out):
    assert x.shape[-1] == D_MODEL and pre_norm_g.shape[0] == 1
    qt, vt, k, ga, qb, kb, vb, gb, la = _in_proj(
        x, pre_norm_g[0], w_in[0], w_gk_up[0], b_gk[0])
    oa = _diff_attn(qt, k, vt, ga, lambda_q1[0], lambda_k1[0], lambda_q2[0],
                    lambda_k2[0], attn_subln_g[0])
    ob = _gla(qb, kb, vb, la, gb, gla_norm_g[0])
    return _out_proj(oa, ob, w_out[0], x, post_norm_g[0])
```

```python
import math

import jax
import jax.numpy as jnp
from jax import lax
from jax.experimental import pallas as pl
from jax.experimental.pallas import tpu as pltpu

D_MODEL = 1024
DA_HEADS = 8
DA_HEAD_DIM = 64
DA_V_DIM = 128
DA_WIDTH = DA_HEADS * DA_V_DIM
GLA_HEADS = 4
GLA_K_DIM = 128
GLA_V_DIM = 256
GLA_QK_WIDTH = GLA_HEADS * GLA_K_DIM
GLA_WIDTH = GLA_HEADS * GLA_V_DIM
GLA_GATE_RANK = 16
GLA_GATE_NORMALIZER = 16.0
GLA_CHUNK = 64
NORM_EPS = 1e-6
LAM_INIT = 0.8 - 0.6 * math.exp(-0.3 * 0)

LANE = 128
BF16_SUBLANES = 16
VMEM_LIMIT = 56 * 1024 * 1024

PROJ_ROWS = 256
ATT_TQ = 1024
ATT_TK = PROJ_ROWS
ATT_VROWS = DA_V_DIM + BF16_SUBLANES
ATT_TN = 256
ATT_UNROLL = 4
GLA_ROWS = 256
OUT_ROWS = 512
NEG_BIG = -1e30

BF16 = jnp.bfloat16
F32 = jnp.float32


def _silu(x):
    return x * (1.0 / (1.0 + jnp.exp(-x)))


def _log_sigmoid(x):
    return jnp.minimum(x, 0.0) - jnp.log(1.0 + jnp.exp(-jnp.abs(x)))


def _in_proj_kernel(x_ref, g_ref, wt_ref, w_ref, wgk_ref, bgk_ref,
                    qt_ref, vt_ref, k_ref, ga_ref, qb_ref, kb_ref, vb_ref,
                    gb_ref, la_ref):
    x = x_ref[0]
    ms = jnp.mean(x * x, axis=-1, keepdims=True)
    h = (x * lax.rsqrt(ms + NORM_EPS) * g_ref[...]).astype(BF16)

    nt = (((1,), (1,)), ((), ()))
    zt = lax.dot_general(wt_ref[...], h, nt, preferred_element_type=F32)
    q_scale = DA_HEAD_DIM ** -0.5 * math.log2(math.e)
    ones = jnp.ones((BF16_SUBLANES, x.shape[0]), BF16)
    for hh in range(DA_HEADS):
        lo = hh * LANE
        qt_ref[0, hh] = (zt[lo:lo + LANE] * q_scale).astype(BF16)
        vt_ref[0, hh, 0, :DA_V_DIM, :] = zt[DA_WIDTH + lo:DA_WIDTH + lo + LANE].astype(BF16)
        vt_ref[0, hh, 0, DA_V_DIM:, :] = ones

    def proj(c0, width):
        return jnp.dot(h, w_ref[:, c0:c0 + width], preferred_element_type=F32)

    zk = proj(0, 1024)
    for hh in range(DA_HEADS):
        k_ref[0, hh] = zk[:, hh * LANE:(hh + 1) * LANE].astype(BF16)
    ga_ref[0] = _silu(proj(1024, 1024)).astype(BF16)
    zqk = proj(2048, 1024)
    qb_ref[0] = (zqk[:, :GLA_QK_WIDTH] * (GLA_K_DIM ** -0.5)).astype(BF16)
    kb_ref[0] = zqk[:, GLA_QK_WIDTH:].astype(BF16)
    vb_ref[0] = proj(3072, 1024).astype(BF16)
    gb_ref[0] = _silu(proj(4096, 1024)).astype(BF16)
    gk_low = proj(5120, LANE).astype(BF16)
    gk = jnp.dot(gk_low, wgk_ref[...], preferred_element_type=F32) + bgk_ref[...]
    la_ref[0] = _log_sigmoid(gk) * (1.0 / GLA_GATE_NORMALIZER)


def _in_proj(x, pre_g, w_in, w_gk_up, b_gk):
    B, S, D = x.shape
    tm = PROJ_ROWS
    c = 0
    cols = {}
    for name, width in (("qa", 1024), ("ka", 1024), ("va", 1024), ("ga", 1024),
                        ("qb", 512), ("kb", 512), ("vb", 1024), ("gb", 1024),
                        ("gk", GLA_GATE_RANK)):
        cols[name] = (c, c + width)
        c += width

    def sl(name):
        a, b = cols[name]
        return w_in[:, a:b]

    wt = jnp.concatenate([sl("qa"), sl("va")], axis=1).T.astype(BF16)
    w = jnp.concatenate(
        [sl("ka"), sl("ga"), sl("qb"), sl("kb"), sl("vb"), sl("gb"),
         jnp.pad(sl("gk"), ((0, 0), (0, LANE - GLA_GATE_RANK)))], axis=1).astype(BF16)
    wgk = jnp.pad(w_gk_up, ((0, LANE - GLA_GATE_RANK), (0, 0))).astype(BF16)
    bgk = b_gk.reshape(1, GLA_QK_WIDTH)
    g = pre_g.reshape(1, D)

    const = dict(pipeline_mode=pl.Buffered(1))
    in_specs = [
        pl.BlockSpec((1, tm, D), lambda b, i: (b, i, 0)),
        pl.BlockSpec((1, D), lambda b, i: (0, 0), **const),
        pl.BlockSpec(wt.shape, lambda b, i: (0, 0), **const),
        pl.BlockSpec(w.shape, lambda b, i: (0, 0), **const),
        pl.BlockSpec(wgk.shape, lambda b, i: (0, 0), **const),
        pl.BlockSpec(bgk.shape, lambda b, i: (0, 0), **const),
    ]
    row3 = lambda width: pl.BlockSpec((1, tm, width), lambda b, i: (b, i, 0))
    out_specs = [
        pl.BlockSpec((1, DA_HEADS, LANE, tm), lambda b, i: (b, 0, 0, i)),
        pl.BlockSpec((1, DA_HEADS, 1, ATT_VROWS, tm), lambda b, i: (b, 0, i, 0, 0)),
        pl.BlockSpec((1, DA_HEADS, tm, LANE), lambda b, i: (b, 0, i, 0)),
        row3(DA_WIDTH),
        row3(GLA_QK_WIDTH), row3(GLA_QK_WIDTH), row3(GLA_WIDTH), row3(GLA_WIDTH),
        row3(GLA_QK_WIDTH),
    ]
    out_shape = [
        jax.ShapeDtypeStruct((B, DA_HEADS, LANE, S), BF16),
        jax.ShapeDtypeStruct((B, DA_HEADS, S // tm, ATT_VROWS, tm), BF16),
        jax.ShapeDtypeStruct((B, DA_HEADS, S, LANE), BF16),
        jax.ShapeDtypeStruct((B, S, DA_WIDTH), BF16),
        jax.ShapeDtypeStruct((B, S, GLA_QK_WIDTH), BF16),
        jax.ShapeDtypeStruct((B, S, GLA_QK_WIDTH), BF16),
        jax.ShapeDtypeStruct((B, S, GLA_WIDTH), BF16),
        jax.ShapeDtypeStruct((B, S, GLA_WIDTH), BF16),
        jax.ShapeDtypeStruct((B, S, GLA_QK_WIDTH), F32),
    ]
    return pl.pallas_call(
        _in_proj_kernel,
        grid=(B, S // tm),
        in_specs=in_specs,
        out_specs=out_specs,
        out_shape=out_shape,
        compiler_params=pltpu.CompilerParams(
            dimension_semantics=("parallel", "parallel"),
            vmem_limit_bytes=VMEM_LIMIT),
        name="in_proj",
    )(x, g, wt, w, wgk, bgk)


def _diff_attn_kernel(qt_ref, k_ref, vt_ref, gate_ref, lq1_ref, lk1_ref, lq2_ref,
                      lk2_ref, g_ref, o_ref, acc_ref, m_ref, s_ref, c_ref):
    tq, tk, tn = ATT_TQ, ATT_TK, ATT_TN
    n_tiles = tq // tn
    qi = pl.program_id(2)
    qt = qt_ref[0, 0]
    row = lax.broadcasted_iota(jnp.int32, qt.shape, 0)
    zero = jnp.zeros_like(qt)
    q_maps = (jnp.where(row < DA_HEAD_DIM, qt, zero),
              jnp.where(row >= DA_HEAD_DIM, qt, zero))
    groups = [(mp, n) for mp in range(2) for n in range(n_tiles)]
    qw = [q_maps[mp][:, n * tn:(n + 1) * tn] for mp, n in groups]
    acc_ref[...] = jnp.zeros_like(acc_ref)
    m_ref[...] = jnp.full_like(m_ref, NEG_BIG)
    causal = (lax.broadcasted_iota(jnp.int32, (tk, tn), 0)
              <= lax.broadcasted_iota(jnp.int32, (tk, tn), 1))

    def key_tile(j):
        return k_ref[0, 0, pl.ds(pl.multiple_of(j * tk, tk), tk), :]

    def issue(k, g, slot, masked):
        s = jnp.dot(k, qw[g], preferred_element_type=F32)
        if masked:
            s = jnp.where(causal, s, NEG_BIG)
        s_ref[slot, g] = s
        c_ref[slot, g] = jnp.max(s, axis=0, keepdims=True)

    def consume(vt, g, slot):
        m_old = m_ref[g]
        m_new = jnp.maximum(m_old, c_ref[slot, g])
        alpha = jnp.exp2(m_old - m_new)
        p = jnp.exp2(s_ref[slot, g] - m_new).astype(BF16)
        acc_ref[g] = alpha * acc_ref[g] + jnp.dot(vt, p, preferred_element_type=F32)
        m_ref[g] = m_new

    def step(k_next, nxt, vt, cur, slot):
        for idx in range(max(len(nxt), len(cur))):
            if idx < len(nxt):
                issue(k_next, nxt[idx][0], 1 - slot, nxt[idx][1])
            if idx < len(cur):
                consume(vt, cur[idx], slot)

    sub = tq // tk
    n_full = qi * sub

    def diag_groups(jj):
        return [(g, n == jj) for g, (_, n) in enumerate(groups) if n >= jj]

    for g, masked in diag_groups(0):
        issue(key_tile(n_full), g, 0, masked)
    for jj in range(sub):
        if jj + 1 < sub:
            k_next, nxt = key_tile(n_full + jj + 1), diag_groups(jj + 1)
        else:
            k_next, nxt = key_tile(0), [(g, False) for g in range(len(groups))]
        step(k_next, nxt, vt_ref[0, 0, n_full + jj], [g for g, _ in diag_groups(jj)], jj % 2)

    all_groups = list(range(len(groups)))

    def full_step(j, slot):
        step(key_tile(j + 1), [(g, False) for g in all_groups], vt_ref[0, 0, j], all_groups, slot)

    def body(i, carry):
        for t in range(ATT_UNROLL):
            full_step(ATT_UNROLL * i + t, t % 2)
        return carry

    lax.fori_loop(0, n_full // ATT_UNROLL, body, 0)

    lam = (jnp.exp(jnp.sum(lq1_ref[...] * lk1_ref[...], axis=-1, keepdims=True))
           - jnp.exp(jnp.sum(lq2_ref[...] * lk2_ref[...], axis=-1, keepdims=True))
           + LAM_INIT)
    for n in range(n_tiles):
        a1, a2 = acc_ref[n], acc_ref[n_tiles + n]
        l1, l2 = a1[DA_V_DIM:DA_V_DIM + 1], a2[DA_V_DIM:DA_V_DIM + 1]
        o = a1[:DA_V_DIM] * (1.0 / l1) - a2[:DA_V_DIM] * (lam / l2)
        msq = jnp.mean(o * o, axis=0, keepdims=True)
        y = o * lax.rsqrt(msq + NORM_EPS) * g_ref[...] * (1.0 - LAM_INIT)
        rows = slice(n * tn, (n + 1) * tn)
        o_ref[0, rows, :] = (y.T * gate_ref[0, rows, :].astype(F32)).astype(BF16)


def _diff_attn(qt, k, vt, gate, lq1, lk1, lq2, lk2, subln_g):
    B, H, _, S = qt.shape
    tq, tk, tn = ATT_TQ, ATT_TK, ATT_TN
    assert tk == tn and ATT_UNROLL % 2 == 0 and (tq // tk) % ATT_UNROLL == 0
    n_col_groups = 2 * (tq // tn)
    lam_spec = pl.BlockSpec((1, DA_HEAD_DIM), lambda b, h, i: (0, 0))
    return pl.pallas_call(
        _diff_attn_kernel,
        grid=(B, H, S // tq),
        in_specs=[
            pl.BlockSpec((1, 1, LANE, tq), lambda b, h, i: (b, h, 0, i)),
            pl.BlockSpec((1, 1, S, LANE), lambda b, h, i: (b, h, 0, 0)),
            pl.BlockSpec((1, 1, S // tk, ATT_VROWS, tk), lambda b, h, i: (b, h, 0, 0, 0)),
            pl.BlockSpec((1, tq, LANE), lambda b, h, i: (b, i, h)),
            lam_spec, lam_spec, lam_spec, lam_spec,
            pl.BlockSpec((DA_V_DIM, 1), lambda b, h, i: (0, 0)),
        ],
        out_specs=pl.BlockSpec((1, tq, LANE), lambda b, h, i: (b, i, h)),
        out_shape=jax.ShapeDtypeStruct((B, S, DA_WIDTH), BF16),
        scratch_shapes=[
            pltpu.VMEM((n_col_groups, ATT_VROWS, tn), F32),
            pltpu.VMEM((n_col_groups, 1, tn), F32),
            pltpu.VMEM((2, n_col_groups, tk, tn), F32),
            pltpu.VMEM((2, n_col_groups, 1, tn), F32),
        ],
        compiler_params=pltpu.CompilerParams(
            dimension_semantics=("parallel", "parallel", "arbitrary"),
            vmem_limit_bytes=VMEM_LIMIT),
        name="diff_attn",
    )(qt, k, vt, gate, lq1.reshape(1, -1), lk1.reshape(1, -1), lq2.reshape(1, -1),
      lk2.reshape(1, -1), subln_g.reshape(DA_V_DIM, 1))


def _gla_kernel(q_ref, k_ref, v_ref, la_ref, gate_ref, g_ref, o_ref, state_ref):
    C = GLA_CHUNK

    @pl.when(pl.program_id(1) == 0)
    def _():
        state_ref[...] = jnp.zeros_like(state_ref)

    r = lax.broadcasted_iota(jnp.int32, (C, C), 0)
    c = lax.broadcasted_iota(jnp.int32, (C, C), 1)
    causal = c <= r
    tri = jnp.where(causal, 1.0, 0.0).astype(BF16)
    tn = (((0,), (0,)), ((), ()))
    nt = (((1,), (1,)), ((), ()))
    chunks = range(GLA_ROWS // C)
    heads = range(GLA_HEADS)
    items = [(ci, h) for ci in chunks for h in heads]
    rows = {ci: slice(ci * C, (ci + 1) * C) for ci in chunks}
    ks = {h: slice(h * GLA_K_DIM, (h + 1) * GLA_K_DIM) for h in heads}
    vs = {h: slice(h * GLA_V_DIM, (h + 1) * GLA_V_DIM) for h in heads}

    b = {}
    for ci, h in items:
        la = la_ref[0, rows[ci], ks[h]]
        la_hi = la.astype(BF16)
        la_lo = (la - la_hi.astype(F32)).astype(BF16)
        b[ci, h] = (jnp.dot(tri, la_hi, preferred_element_type=F32)
                    + jnp.dot(tri, la_lo, preferred_element_type=F32))
    v = {(ci, h): v_ref[0, rows[ci], vs[h]] for ci, h in items}
    q_dec, k_inv, k_tail, decay = {}, {}, {}, {}
    for ci, h in items:
        bb = b[ci, h]
        b_last = bb[C - 1:C, :]
        q = q_ref[0, rows[ci], ks[h]].astype(F32)
        k = k_ref[0, rows[ci], ks[h]].astype(F32)
        q_dec[ci, h] = (q * jnp.exp(bb)).astype(BF16)
        k_inv[ci, h] = (k * jnp.exp(-bb)).astype(BF16)
        k_tail[ci, h] = (k * jnp.exp(b_last - bb)).astype(BF16)
        decay[ci, h] = jnp.exp(b_last)
    attn = {i: jnp.where(causal, lax.dot_general(q_dec[i], k_inv[i], nt,
                                                 preferred_element_type=F32), 0.0).astype(BF16)
            for i in items}
    kv_t = {i: lax.dot_general(v[i], k_tail[i], tn, preferred_element_type=F32)
            for i in items}
    intra = {i: jnp.dot(attn[i], v[i], preferred_element_type=F32) for i in items}

    for ci in chunks:
        o = {}
        for h in heads:
            state_t = state_ref[h]
            o[h] = intra[ci, h] + lax.dot_general(q_dec[ci, h], state_t.astype(BF16), nt,
                                                  preferred_element_type=F32)
            state_ref[h] = state_t * decay[ci, h] + kv_t[ci, h]
        for h in heads:
            ms = jnp.mean(o[h] * o[h], axis=-1, keepdims=True)
            y = o[h] * lax.rsqrt(ms + NORM_EPS) * g_ref[...]
            o_ref[0, rows[ci], vs[h]] = (y * gate_ref[0, rows[ci], vs[h]].astype(F32)).astype(BF16)


def _gla(qb, kb, vb, la, gate, gla_g):
    B, S, _ = qb.shape
    tg = GLA_ROWS
    row3 = lambda width: pl.BlockSpec((1, tg, width), lambda b, i: (b, i, 0))
    return pl.pallas_call(
        _gla_kernel,
        grid=(B, S // tg),
        in_specs=[row3(GLA_QK_WIDTH), row3(GLA_QK_WIDTH), row3(GLA_WIDTH),
                  row3(GLA_QK_WIDTH), row3(GLA_WIDTH),
                  pl.BlockSpec((1, GLA_V_DIM), lambda b, i: (0, 0))],
        out_specs=row3(GLA_WIDTH),
        out_shape=jax.ShapeDtypeStruct((B, S, GLA_WIDTH), BF16),
        scratch_shapes=[pltpu.VMEM((GLA_HEADS, GLA_V_DIM, GLA_K_DIM), F32)],
        compiler_params=pltpu.CompilerParams(
            dimension_semantics=("parallel", "arbitrary"),
            vmem_limit_bytes=VMEM_LIMIT),
        name="gla",
    )(qb, kb, vb, la, gate, gla_g.reshape(1, GLA_V_DIM))


def _out_proj_kernel(oa_ref, ob_ref, w_ref, x_ref, g_ref, o_ref):
    y = (jnp.dot(oa_ref[0], w_ref[:DA_WIDTH, :], preferred_element_type=F32)
         + jnp.dot(ob_ref[0], w_ref[DA_WIDTH:, :], preferred_element_type=F32))
    ms = jnp.mean(y * y, axis=-1, keepdims=True)
    o_ref[0] = x_ref[0] + y * lax.rsqrt(ms + NORM_EPS) * g_ref[...]


def _out_proj(oa, ob, w_out, x, post_g):
    B, S, D = x.shape
    tm = OUT_ROWS
    row3 = lambda width: pl.BlockSpec((1, tm, width), lambda b, i: (b, i, 0))
    const = dict(pipeline_mode=pl.Buffered(1))
    return pl.pallas_call(
        _out_proj_kernel,
        grid=(B, S // tm),
        in_specs=[row3(DA_WIDTH), row3(GLA_WIDTH),
                  pl.BlockSpec(w_out.shape, lambda b, i: (0, 0), **const),
                  row3(D),
                  pl.BlockSpec((1, D), lambda b, i: (0, 0), **const)],
        out_specs=row3(D),
        out_shape=jax.ShapeDtypeStruct((B, S, D), F32),
        compiler_params=pltpu.CompilerParams(
            dimension_semantics=("parallel", "parallel"),
            vmem_limit_bytes=VMEM_LIMIT),
        name="out_proj",
    )(oa, ob, w_out.astype(BF16), x, post_g.reshape(1, D))


def kernel(x, pre_norm_g, post_norm_g, w_in, w_gk_up, b_gk, lambda_q1, lambda_k1,
           lambda_q2, lambda_k2, attn_subln_g, gla_norm_g, w_out):
    assert x.shape[-1] == D_MODEL and pre_norm_g.shape[0] == 1
    qt, vt, k, ga, qb, kb, vb, gb, la = _in_proj(
        x, pre_norm_g[0], w_in[0], w_gk_up[0], b_gk[0])
    oa = _diff_attn(qt, k, vt, ga, lambda_q1[0], lambda_k1[0], lambda_q2[0],
                    lambda_k2[0], attn_subln_g[0])
    ob = _gla(qb, kb, vb, la, gb, gla_norm_g[0])
    return _out_proj(oa, ob, w_out[0], x, post_norm_g[0])
```

```python
import math

import jax
import jax.numpy as jnp
from jax import lax
from jax.experimental import pallas as pl
from jax.experimental.pallas import tpu as pltpu

D_MODEL = 1024
DA_HEADS = 8
DA_HEAD_DIM = 64
DA_V_DIM = 128
DA_WIDTH = DA_HEADS * DA_V_DIM
GLA_HEADS = 4
GLA_K_DIM = 128
GLA_V_DIM = 256
GLA_QK_WIDTH = GLA_HEADS * GLA_K_DIM
GLA_WIDTH = GLA_HEADS * GLA_V_DIM
GLA_GATE_RANK = 16
GLA_GATE_NORMALIZER = 16.0
GLA_CHUNK = 64
NORM_EPS = 1e-6
LAM_INIT = 0.8 - 0.6 * math.exp(-0.3 * 0)

LANE = 128
BF16_SUBLANES = 16
VMEM_LIMIT = 56 * 1024 * 1024

PROJ_ROWS = 512
ATT_TQ = 1024
ATT_TK = 256
ATT_VROWS = DA_V_DIM + BF16_SUBLANES
ATT_TN = 256
ATT_UNROLL = 4
GLA_ROWS = 512
OUT_ROWS = 1024
NEG_BIG = -1e30

BF16 = jnp.bfloat16
F32 = jnp.float32


def _silu(x):
    return x * (1.0 / (1.0 + jnp.exp(-x)))


def _log_sigmoid(x):
    return jnp.minimum(x, 0.0) - jnp.log(1.0 + jnp.exp(-jnp.abs(x)))


def _in_proj_kernel(x_ref, g_ref, wt_ref, w_ref, wgk_ref, bgk_ref,
                    qt_ref, vt_ref, k_ref, ga_ref, qb_ref, kb_ref, vb_ref,
                    gb_ref, la_ref):
    x = x_ref[0]
    ms = jnp.mean(x * x, axis=-1, keepdims=True)
    h = (x * lax.rsqrt(ms + NORM_EPS) * g_ref[...]).astype(BF16)

    nt = (((1,), (1,)), ((), ()))
    zt = lax.dot_general(wt_ref[...], h, nt, preferred_element_type=F32)
    q_scale = DA_HEAD_DIM ** -0.5 * math.log2(math.e)
    ones = jnp.ones((BF16_SUBLANES, ATT_TK), BF16)
    for hh in range(DA_HEADS):
        lo = hh * LANE
        for blk in range(x.shape[0] // ATT_TK):
            cs = slice(blk * ATT_TK, (blk + 1) * ATT_TK)
            qt_ref[0, hh, blk] = (zt[lo:lo + LANE, cs] * q_scale).astype(BF16)
            vt_ref[0, hh, blk, :DA_V_DIM, :] = zt[DA_WIDTH + lo:DA_WIDTH + lo + LANE, cs].astype(BF16)
            vt_ref[0, hh, blk, DA_V_DIM:, :] = ones

    def proj(c0, width):
        return jnp.dot(h, w_ref[:, c0:c0 + width], preferred_element_type=F32)

    zk = proj(0, 1024)
    for hh in range(DA_HEADS):
        k_ref[0, hh] = zk[:, hh * LANE:(hh + 1) * LANE].astype(BF16)
    ga_ref[0] = _silu(proj(1024, 1024)).astype(BF16)
    zqk = proj(2048, 1024)
    qb_ref[0] = (zqk[:, :GLA_QK_WIDTH] * (GLA_K_DIM ** -0.5)).astype(BF16)
    kb_ref[0] = zqk[:, GLA_QK_WIDTH:].astype(BF16)
    vb_ref[0] = proj(3072, 1024).astype(BF16)
    gb_ref[0] = _silu(proj(4096, 1024)).astype(BF16)
    gk_low = proj(5120, LANE).astype(BF16)
    gk = jnp.dot(gk_low, wgk_ref[...], preferred_element_type=F32) + bgk_ref[...]
    la_ref[0] = _log_sigmoid(gk) * (1.0 / GLA_GATE_NORMALIZER)


def _in_proj(x, pre_g, w_in, w_gk_up, b_gk):
    B, S, D = x.shape
    tm = PROJ_ROWS
    nblk = tm // ATT_TK
    c = 0
    cols = {}
    for name, width in (("qa", 1024), ("ka", 1024), ("va", 1024), ("ga", 1024),
                        ("qb", 512), ("kb", 512), ("vb", 1024), ("gb", 1024),
                        ("gk", GLA_GATE_RANK)):
        cols[name] = (c, c + width)
        c += width

    def sl(name):
        a, b = cols[name]
        return w_in[:, a:b]

    wt = jnp.concatenate([sl("qa"), sl("va")], axis=1).T.astype(BF16)
    w = jnp.concatenate(
        [sl("ka"), sl("ga"), sl("qb"), sl("kb"), sl("vb"), sl("gb"),
         jnp.pad(sl("gk"), ((0, 0), (0, LANE - GLA_GATE_RANK)))], axis=1).astype(BF16)
    wgk = jnp.pad(w_gk_up, ((0, LANE - GLA_GATE_RANK), (0, 0))).astype(BF16)
    bgk = b_gk.reshape(1, GLA_QK_WIDTH)
    g = pre_g.reshape(1, D)

    const = dict(pipeline_mode=pl.Buffered(1))
    in_specs = [
        pl.BlockSpec((1, tm, D), lambda b, i: (b, i, 0)),
        pl.BlockSpec((1, D), lambda b, i: (0, 0), **const),
        pl.BlockSpec(wt.shape, lambda b, i: (0, 0), **const),
        pl.BlockSpec(w.shape, lambda b, i: (0, 0), **const),
        pl.BlockSpec(wgk.shape, lambda b, i: (0, 0), **const),
        pl.BlockSpec(bgk.shape, lambda b, i: (0, 0), **const),
    ]
    row3 = lambda width: pl.BlockSpec((1, tm, width), lambda b, i: (b, i, 0))
    out_specs = [
        pl.BlockSpec((1, DA_HEADS, nblk, LANE, ATT_TK), lambda b, i: (b, 0, i, 0, 0)),
        pl.BlockSpec((1, DA_HEADS, nblk, ATT_VROWS, ATT_TK), lambda b, i: (b, 0, i, 0, 0)),
        pl.BlockSpec((1, DA_HEADS, tm, LANE), lambda b, i: (b, 0, i, 0)),
        row3(DA_WIDTH),
        row3(GLA_QK_WIDTH), row3(GLA_QK_WIDTH), row3(GLA_WIDTH), row3(GLA_WIDTH),
        row3(GLA_QK_WIDTH),
    ]
    out_shape = [
        jax.ShapeDtypeStruct((B, DA_HEADS, S // ATT_TK, LANE, ATT_TK), BF16),
        jax.ShapeDtypeStruct((B, DA_HEADS, S // ATT_TK, ATT_VROWS, ATT_TK), BF16),
        jax.ShapeDtypeStruct((B, DA_HEADS, S, LANE), BF16),
        jax.ShapeDtypeStruct((B, S, DA_WIDTH), BF16),
        jax.ShapeDtypeStruct((B, S, GLA_QK_WIDTH), BF16),
        jax.ShapeDtypeStruct((B, S, GLA_QK_WIDTH), BF16),
        jax.ShapeDtypeStruct((B, S, GLA_WIDTH), BF16),
        jax.ShapeDtypeStruct((B, S, GLA_WIDTH), BF16),
        jax.ShapeDtypeStruct((B, S, GLA_QK_WIDTH), F32),
    ]
    return pl.pallas_call(
        _in_proj_kernel,
        grid=(B, S // tm),
        in_specs=in_specs,
        out_specs=out_specs,
        out_shape=out_shape,
        compiler_params=pltpu.CompilerParams(
            dimension_semantics=("parallel", "parallel"),
            vmem_limit_bytes=VMEM_LIMIT),
        name="in_proj",
    )(x, g, wt, w, wgk, bgk)


def _diff_attn_kernel(qt_ref, k_ref, vt_ref, gate_ref, lq1_ref, lk1_ref, lq2_ref,
                      lk2_ref, g_ref, o_ref, acc_ref, m_ref, s_ref, c_ref):
    tq, tk, tn = ATT_TQ, ATT_TK, ATT_TN
    n_tiles = tq // tn
    sub = tq // tk
    n_q = o_ref.shape[1] // tq
    groups = [(mp, n) for mp in range(2) for n in range(n_tiles)]
    all_groups = [(g, False) for g in range(len(groups))]
    row = lax.broadcasted_iota(jnp.int32, (LANE, tn), 0)
    map_rows = (row < DA_HEAD_DIM, row >= DA_HEAD_DIM)
    causal = (lax.broadcasted_iota(jnp.int32, (tk, tn), 0)
              <= lax.broadcasted_iota(jnp.int32, (tk, tn), 1))
    lam = (jnp.exp(jnp.sum(lq1_ref[...] * lk1_ref[...], axis=-1, keepdims=True))
           - jnp.exp(jnp.sum(lq2_ref[...] * lk2_ref[...], axis=-1, keepdims=True))
           + LAM_INIT)

    def aligned(start, size):
        return pl.ds(start if isinstance(start, int) else pl.multiple_of(start, size), size)

    def key_tile(j):
        return k_ref[0, 0, aligned(j * tk, tk), :]

    def epilogue(qi):
        for n in range(n_tiles):
            a1, a2 = acc_ref[n], acc_ref[n_tiles + n]
            l1, l2 = a1[DA_V_DIM:DA_V_DIM + 1], a2[DA_V_DIM:DA_V_DIM + 1]
            o = a1[:DA_V_DIM] * (1.0 / l1) - a2[:DA_V_DIM] * (lam / l2)
            msq = jnp.mean(o * o, axis=0, keepdims=True)
            y = o * lax.rsqrt(msq + NORM_EPS) * g_ref[...] * (1.0 - LAM_INIT)
            rows = aligned(qi * tq + n * tn, tn)
            o_ref[0, rows, :] = (y.T * gate_ref[0, rows, :].astype(F32)).astype(BF16)

    def query_tile(qi, finish_previous):
        qw = []
        for mp, n in groups:
            blk = qt_ref[0, 0, qi * n_tiles + n]
            qw.append(jnp.where(map_rows[mp], blk, jnp.zeros_like(blk)))

        def issue(k, g, slot, masked):
            s = jnp.dot(k, qw[g], preferred_element_type=F32)
            if masked:
                s = jnp.where(causal, s, NEG_BIG)
            s_ref[slot, g] = s
            c_ref[slot, g] = jnp.max(s, axis=0, keepdims=True)

        def consume(vt, g, slot, mask_now):
            s, c = s_ref[slot, g], c_ref[slot, g]
            if mask_now:
                s = jnp.where(causal, s, NEG_BIG)
                c = jnp.max(s, axis=0, keepdims=True)
            m_old = m_ref[g]
            m_new = jnp.maximum(m_old, c)
            alpha = jnp.exp2(m_old - m_new)
            p = jnp.exp2(s - m_new).astype(BF16)
            acc_ref[g] = alpha * acc_ref[g] + jnp.dot(vt, p, preferred_element_type=F32)
            m_ref[g] = m_new

        def step(k_next, nxt, vt, cur, slot):
            for idx in range(max(len(nxt), len(cur))):
                if idx < len(nxt):
                    issue(k_next, nxt[idx][0], 1 - slot, nxt[idx][1])
                if idx < len(cur):
                    consume(vt, cur[idx][0], slot, cur[idx][1])

        n_full = qi * sub
        k_first = key_tile(0)
        for g, _ in all_groups:
            issue(k_first, g, 0, False)
        if finish_previous:
            epilogue(qi - 1)
        acc_ref[...] = jnp.zeros_like(acc_ref)
        m_ref[...] = jnp.full_like(m_ref, NEG_BIG)

        def body(i, carry):
            for t in range(ATT_UNROLL):
                j = ATT_UNROLL * i + t
                step(key_tile(j + 1), all_groups, vt_ref[0, 0, j], all_groups, t % 2)
            return carry

        if not isinstance(qi, int) or qi > 0:
            lax.fori_loop(0, n_full // ATT_UNROLL, body, 0)

        def diag_groups(jj):
            return [(g, n == jj) for g, (_, n) in enumerate(groups) if n >= jj]

        for jj in range(sub):
            cur = diag_groups(jj) if jj == 0 else [(g, False) for g, _ in diag_groups(jj)]
            nxt = diag_groups(jj + 1) if jj + 1 < sub else []
            k_next = key_tile(n_full + jj + 1) if nxt else None
            step(k_next, nxt, vt_ref[0, 0, n_full + jj], cur, jj % 2)

    query_tile(0, False)

    def outer(qi, carry):
        query_tile(qi, True)
        return carry

    lax.fori_loop(1, n_q, outer, 0)
    epilogue(n_q - 1)


def _diff_attn(qt, k, vt, gate, lq1, lk1, lq2, lk2, subln_g):
    B, H, S, _ = k.shape
    tq, tk, tn = ATT_TQ, ATT_TK, ATT_TN
    assert tk == tn and ATT_UNROLL % 2 == 0 and (tq // tk) % ATT_UNROLL == 0
    n_col_groups = 2 * (tq // tn)
    lam_spec = pl.BlockSpec((1, DA_HEAD_DIM), lambda b, h: (0, 0))
    return pl.pallas_call(
        _diff_attn_kernel,
        grid=(B, H),
        in_specs=[
            pl.BlockSpec((1, 1, S // tn, LANE, tn), lambda b, h: (b, h, 0, 0, 0)),
            pl.BlockSpec((1, 1, S, LANE), lambda b, h: (b, h, 0, 0)),
            pl.BlockSpec((1, 1, S // tk, ATT_VROWS, tk), lambda b, h: (b, h, 0, 0, 0)),
            pl.BlockSpec((1, S, LANE), lambda b, h: (b, 0, h)),
            lam_spec, lam_spec, lam_spec, lam_spec,
            pl.BlockSpec((DA_V_DIM, 1), lambda b, h: (0, 0)),
        ],
        out_specs=pl.BlockSpec((1, S, LANE), lambda b, h: (b, 0, h)),
        out_shape=jax.ShapeDtypeStruct((B, S, DA_WIDTH), BF16),
        scratch_shapes=[
            pltpu.VMEM((n_col_groups, ATT_VROWS, tn), F32),
            pltpu.VMEM((n_col_groups, 1, tn), F32),
            pltpu.VMEM((2, n_col_groups, tk, tn), F32),
            pltpu.VMEM((2, n_col_groups, 1, tn), F32),
        ],
        compiler_params=pltpu.CompilerParams(
            dimension_semantics=("parallel", "parallel"),
            vmem_limit_bytes=VMEM_LIMIT),
        name="diff_attn",
    )(qt, k, vt, gate, lq1.reshape(1, -1), lk1.reshape(1, -1), lq2.reshape(1, -1),
      lk2.reshape(1, -1), subln_g.reshape(DA_V_DIM, 1))


def _gla_kernel(q_ref, k_ref, v_ref, la_ref, gate_ref, g_ref, o_ref, state_ref):
    C = GLA_CHUNK

    @pl.when(pl.program_id(1) == 0)
    def _():
        state_ref[...] = jnp.zeros_like(state_ref)

    r = lax.broadcasted_iota(jnp.int32, (C, C), 0)
    c = lax.broadcasted_iota(jnp.int32, (C, C), 1)
    causal = c <= r
    tri = jnp.where(causal, 1.0, 0.0).astype(BF16)
    tn = (((0,), (0,)), ((), ()))
    nt = (((1,), (1,)), ((), ()))
    chunks = range(GLA_ROWS // C)
    heads = range(GLA_HEADS)
    items = [(ci, h) for ci in chunks for h in heads]
    rows = {ci: slice(ci * C, (ci + 1) * C) for ci in chunks}
    ks = {h: slice(h * GLA_K_DIM, (h + 1) * GLA_K_DIM) for h in heads}
    vs = {h: slice(h * GLA_V_DIM, (h + 1) * GLA_V_DIM) for h in heads}

    b = {}
    for ci, h in items:
        la = la_ref[0, rows[ci], ks[h]]
        la_hi = la.astype(BF16)
        la_lo = (la - la_hi.astype(F32)).astype(BF16)
        b[ci, h] = (jnp.dot(tri, la_hi, preferred_element_type=F32)
                    + jnp.dot(tri, la_lo, preferred_element_type=F32))
    v = {(ci, h): v_ref[0, rows[ci], vs[h]] for ci, h in items}
    q_dec, k_inv, k_tail, decay = {}, {}, {}, {}
    for ci, h in items:
        bb = b[ci, h]
        b_last = bb[C - 1:C, :]
        q = q_ref[0, rows[ci], ks[h]].astype(F32)
        k = k_ref[0, rows[ci], ks[h]].astype(F32)
        q_dec[ci, h] = (q * jnp.exp(bb)).astype(BF16)
        k_inv[ci, h] = (k * jnp.exp(-bb)).astype(BF16)
        k_tail[ci, h] = (k * jnp.exp(b_last - bb)).astype(BF16)
        decay[ci, h] = jnp.exp(b_last)
    attn = {i: jnp.where(causal, lax.dot_general(q_dec[i], k_inv[i], nt,
                                                 preferred_element_type=F32), 0.0).astype(BF16)
            for i in items}
    kv_t = {i: lax.dot_general(v[i], k_tail[i], tn, preferred_element_type=F32)
            for i in items}
    intra = {i: jnp.dot(attn[i], v[i], preferred_element_type=F32) for i in items}

    for ci in chunks:
        o = {}
        for h in heads:
            state_t = state_ref[h]
            o[h] = intra[ci, h] + lax.dot_general(q_dec[ci, h], state_t.astype(BF16), nt,
                                                  preferred_element_type=F32)
            state_ref[h] = state_t * decay[ci, h] + kv_t[ci, h]
        for h in heads:
            ms = jnp.mean(o[h] * o[h], axis=-1, keepdims=True)
            y = o[h] * lax.rsqrt(ms + NORM_EPS) * g_ref[...]
            o_ref[0, rows[ci], vs[h]] = (y * gate_ref[0, rows[ci], vs[h]].astype(F32)).astype(BF16)


def _gla(qb, kb, vb, la, gate, gla_g):
    B, S, _ = qb.shape
    tg = GLA_ROWS
    row3 = lambda width: pl.BlockSpec((1, tg, width), lambda b, i: (b, i, 0))
    return pl.pallas_call(
        _gla_kernel,
        grid=(B, S // tg),
        in_specs=[row3(GLA_QK_WIDTH), row3(GLA_QK_WIDTH), row3(GLA_WIDTH),
                  row3(GLA_QK_WIDTH), row3(GLA_WIDTH),
                  pl.BlockSpec((1, GLA_V_DIM), lambda b, i: (0, 0))],
        out_specs=row3(GLA_WIDTH),
        out_shape=jax.ShapeDtypeStruct((B, S, GLA_WIDTH), BF16),
        scratch_shapes=[pltpu.VMEM((GLA_HEADS, GLA_V_DIM, GLA_K_DIM), F32)],
        compiler_params=pltpu.CompilerParams(
            dimension_semantics=("parallel", "arbitrary"),
            vmem_limit_bytes=VMEM_LIMIT),
        name="gla",
    )(qb, kb, vb, la, gate, gla_g.reshape(1, GLA_V_DIM))


def _out_proj_kernel(oa_ref, ob_ref, w_ref, x_ref, g_ref, o_ref):
    y = (jnp.dot(oa_ref[0], w_ref[:DA_WIDTH, :], preferred_element_type=F32)
         + jnp.dot(ob_ref[0], w_ref[DA_WIDTH:, :], preferred_element_type=F32))
    ms = jnp.mean(y * y, axis=-1, keepdims=True)
    o_ref[0] = x_ref[0] + y * lax.rsqrt(ms + NORM_EPS) * g_ref[...]


def _out_proj(oa, ob, w_out, x, post_g):
    B, S, D = x.shape
    tm = OUT_ROWS
    row3 = lambda width: pl.BlockSpec((1, tm, width), lambda b, i: (b, i, 0))
    const = dict(pipeline_mode=pl.Buffered(1))
    return pl.pallas_call(
        _out_proj_kernel,
        grid=(B, S // tm),
        in_specs=[row3(DA_WIDTH), row3(GLA_WIDTH),
                  pl.BlockSpec(w_out.shape, lambda b, i: (0, 0), **const),
                  row3(D),
                  pl.BlockSpec((1, D), lambda b, i: (0, 0), **const)],
        out_specs=row3(D),
        out_shape=jax.ShapeDtypeStruct((B, S, D), F32),
        compiler_params=pltpu.CompilerParams(
            dimension_semantics=("parallel", "parallel"),
            vmem_limit_bytes=VMEM_LIMIT),
        name="out_proj",
    )(oa, ob, w_out.astype(BF16), x, post_g.reshape(1, D))


def kernel(x, pre_norm_g, post_norm_g, w_in, w_gk_up, b_gk, lambda_q1, lambda_k1,
           lambda_q2, lambda_k2, attn_subln_g, gla_norm_g, w_out):
    assert x.shape[-1] == D_MODEL and pre_norm_g.shape[0] == 1
    qt, vt, k, ga, qb, kb, vb, gb, la = _in_proj(
        x, pre_norm_g[0], w_in[0], w_gk_up[0], b_gk[0])
    oa = _diff_attn(qt, k, vt, ga, lambda_q1[0], lambda_k1[0], lambda_q2[0],
                    lambda_k2[0], attn_subln_g[0])
    ob = _gla(qb, kb, vb, la, gb, gla_norm_g[0])
    return _out_proj(oa, ob, w_out[0], x, post_norm_g[0])
```

```python
import math

import jax
import jax.numpy as jnp
from jax import lax
from jax.experimental import pallas as pl
from jax.experimental.pallas import tpu as pltpu

D_MODEL = 1024
DA_HEADS = 8
DA_HEAD_DIM = 64
DA_V_DIM = 128
DA_WIDTH = DA_HEADS * DA_V_DIM
GLA_HEADS = 4
GLA_K_DIM = 128
GLA_V_DIM = 256
GLA_QK_WIDTH = GLA_HEADS * GLA_K_DIM
GLA_WIDTH = GLA_HEADS * GLA_V_DIM
GLA_GATE_RANK = 16
GLA_GATE_NORMALIZER = 16.0
GLA_CHUNK = 64
NORM_EPS = 1e-6
LAM_INIT = 0.8 - 0.6 * math.exp(-0.3 * 0)

LANE = 128
BF16_SUBLANES = 16
VMEM_LIMIT = 56 * 1024 * 1024

PROJ_ROWS = 512
ATT_TQ = 2048
ATT_TK = 256
ATT_VROWS = DA_V_DIM + BF16_SUBLANES
ATT_TN = 256
ATT_UNROLL = 2
GLA_ROWS = 512
OUT_ROWS = 1024
NEG_BIG = -1e30

BF16 = jnp.bfloat16
F32 = jnp.float32


def _silu(x):
    return x * (1.0 / (1.0 + jnp.exp(-x)))


def _log_sigmoid(x):
    return jnp.minimum(x, 0.0) - jnp.log(1.0 + jnp.exp(-jnp.abs(x)))


def _in_proj_kernel(x_ref, g_ref, wt_ref, w_ref, wgk_ref, bgk_ref,
                    qt_ref, vt_ref, k_ref, ga_ref, qb_ref, kb_ref, vb_ref,
                    gb_ref, la_ref):
    x = x_ref[0]
    ms = jnp.mean(x * x, axis=-1, keepdims=True)
    h = (x * lax.rsqrt(ms + NORM_EPS) * g_ref[...]).astype(BF16)

    nt = (((1,), (1,)), ((), ()))
    zt = lax.dot_general(wt_ref[...], h, nt, preferred_element_type=F32)
    q_scale = DA_HEAD_DIM ** -0.5 * math.log2(math.e)
    ones = jnp.ones((BF16_SUBLANES, ATT_TK), BF16)
    for hh in range(DA_HEADS):
        lo = hh * LANE
        for blk in range(x.shape[0] // ATT_TK):
            cs = slice(blk * ATT_TK, (blk + 1) * ATT_TK)
            qt_ref[0, hh, blk] = (zt[lo:lo + LANE, cs] * q_scale).astype(BF16)
            vt_ref[0, hh, blk, :DA_V_DIM, :] = zt[DA_WIDTH + lo:DA_WIDTH + lo + LANE, cs].astype(BF16)
            vt_ref[0, hh, blk, DA_V_DIM:, :] = ones

    def proj(c0, width):
        return jnp.dot(h, w_ref[:, c0:c0 + width], preferred_element_type=F32)

    zk = proj(0, 1024)
    for hh in range(DA_HEADS):
        k_ref[0, hh] = zk[:, hh * LANE:(hh + 1) * LANE].astype(BF16)
    ga_ref[0] = _silu(proj(1024, 1024)).astype(BF16)
    zqk = proj(2048, 1024)
    qb_ref[0] = (zqk[:, :GLA_QK_WIDTH] * (GLA_K_DIM ** -0.5)).astype(BF16)
    kb_ref[0] = zqk[:, GLA_QK_WIDTH:].astype(BF16)
    vb_ref[0] = proj(3072, 1024).astype(BF16)
    gb_ref[0] = _silu(proj(4096, 1024)).astype(BF16)
    gk_low = proj(5120, LANE).astype(BF16)
    gk = jnp.dot(gk_low, wgk_ref[...], preferred_element_type=F32) + bgk_ref[...]
    la_ref[0] = _log_sigmoid(gk) * (math.log2(math.e) / GLA_GATE_NORMALIZER)


def _in_proj(x, pre_g, w_in, w_gk_up, b_gk):
    B, S, D = x.shape
    tm = PROJ_ROWS
    nblk = tm // ATT_TK
    c = 0
    cols = {}
    for name, width in (("qa", 1024), ("ka", 1024), ("va", 1024), ("ga", 1024),
                        ("qb", 512), ("kb", 512), ("vb", 1024), ("gb", 1024),
                        ("gk", GLA_GATE_RANK)):
        cols[name] = (c, c + width)
        c += width

    def sl(name):
        a, b = cols[name]
        return w_in[:, a:b]

    wt = jnp.concatenate([sl("qa"), sl("va")], axis=1).T.astype(BF16)
    w = jnp.concatenate(
        [sl("ka"), sl("ga"), sl("qb"), sl("kb"), sl("vb"), sl("gb"),
         jnp.pad(sl("gk"), ((0, 0), (0, LANE - GLA_GATE_RANK)))], axis=1).astype(BF16)
    wgk = jnp.pad(w_gk_up, ((0, LANE - GLA_GATE_RANK), (0, 0))).astype(BF16)
    bgk = b_gk.reshape(1, GLA_QK_WIDTH)
    g = pre_g.reshape(1, D)

    const = dict(pipeline_mode=pl.Buffered(1))
    in_specs = [
        pl.BlockSpec((1, tm, D), lambda b, i: (b, i, 0)),
        pl.BlockSpec((1, D), lambda b, i: (0, 0), **const),
        pl.BlockSpec(wt.shape, lambda b, i: (0, 0), **const),
        pl.BlockSpec(w.shape, lambda b, i: (0, 0), **const),
        pl.BlockSpec(wgk.shape, lambda b, i: (0, 0), **const),
        pl.BlockSpec(bgk.shape, lambda b, i: (0, 0), **const),
    ]
    row3 = lambda width: pl.BlockSpec((1, tm, width), lambda b, i: (b, i, 0))
    out_specs = [
        pl.BlockSpec((1, DA_HEADS, nblk, LANE, ATT_TK), lambda b, i: (b, 0, i, 0, 0)),
        pl.BlockSpec((1, DA_HEADS, nblk, ATT_VROWS, ATT_TK), lambda b, i: (b, 0, i, 0, 0)),
        pl.BlockSpec((1, DA_HEADS, tm, LANE), lambda b, i: (b, 0, i, 0)),
        row3(DA_WIDTH),
        row3(GLA_QK_WIDTH), row3(GLA_QK_WIDTH), row3(GLA_WIDTH), row3(GLA_WIDTH),
        row3(GLA_QK_WIDTH),
    ]
    out_shape = [
        jax.ShapeDtypeStruct((B, DA_HEADS, S // ATT_TK, LANE, ATT_TK), BF16),
        jax.ShapeDtypeStruct((B, DA_HEADS, S // ATT_TK, ATT_VROWS, ATT_TK), BF16),
        jax.ShapeDtypeStruct((B, DA_HEADS, S, LANE), BF16),
        jax.ShapeDtypeStruct((B, S, DA_WIDTH), BF16),
        jax.ShapeDtypeStruct((B, S, GLA_QK_WIDTH), BF16),
        jax.ShapeDtypeStruct((B, S, GLA_QK_WIDTH), BF16),
        jax.ShapeDtypeStruct((B, S, GLA_WIDTH), BF16),
        jax.ShapeDtypeStruct((B, S, GLA_WIDTH), BF16),
        jax.ShapeDtypeStruct((B, S, GLA_QK_WIDTH), F32),
    ]
    return pl.pallas_call(
        _in_proj_kernel,
        grid=(B, S // tm),
        in_specs=in_specs,
        out_specs=out_specs,
        out_shape=out_shape,
        compiler_params=pltpu.CompilerParams(
            dimension_semantics=("parallel", "parallel"),
            vmem_limit_bytes=VMEM_LIMIT),
        name="in_proj",
    )(x, g, wt, w, wgk, bgk)


def _diff_attn_kernel(qt_ref, k_ref, vt_ref, gate_ref, lq1_ref, lk1_ref, lq2_ref,
                      lk2_ref, g_ref, o_ref, acc_ref, m_ref, s_ref, c_ref):
    tq, tk, tn = ATT_TQ, ATT_TK, ATT_TN
    n_tiles = tq // tn
    sub = tq // tk
    n_q = o_ref.shape[1] // tq
    groups = [(mp, n) for mp in range(2) for n in range(n_tiles)]
    all_groups = [(g, False) for g in range(len(groups))]
    row = lax.broadcasted_iota(jnp.int32, (LANE, tn), 0)
    map_rows = (row < DA_HEAD_DIM, row >= DA_HEAD_DIM)
    causal = (lax.broadcasted_iota(jnp.int32, (tk, tn), 0)
              <= lax.broadcasted_iota(jnp.int32, (tk, tn), 1))
    lam = (jnp.exp(jnp.sum(lq1_ref[...] * lk1_ref[...], axis=-1, keepdims=True))
           - jnp.exp(jnp.sum(lq2_ref[...] * lk2_ref[...], axis=-1, keepdims=True))
           + LAM_INIT)
    gain = jnp.broadcast_to(g_ref[...] * (1.0 - LAM_INIT), (DA_V_DIM, tn))

    def aligned(start, size):
        return pl.ds(start if isinstance(start, int) else pl.multiple_of(start, size), size)

    def key_tile(j):
        return k_ref[0, 0, aligned(j * tk, tk), :]

    def epilogue(qi):
        for n in range(n_tiles):
            a1, a2 = acc_ref[n], acc_ref[n_tiles + n]
            l1, l2 = a1[DA_V_DIM:DA_V_DIM + 1], a2[DA_V_DIM:DA_V_DIM + 1]
            o = a1[:DA_V_DIM] * (1.0 / l1) - a2[:DA_V_DIM] * (lam / l2)
            msq = jnp.mean(o * o, axis=0, keepdims=True)
            y = o * lax.rsqrt(msq + NORM_EPS) * gain
            rows = aligned(qi * tq + n * tn, tn)
            o_ref[0, rows, :] = (y.T * gate_ref[0, rows, :].astype(F32)).astype(BF16)

    def q_weights(qi):
        out = []
        for mp, n in groups:
            blk = qt_ref[0, 0, qi * n_tiles + n]
            out.append(jnp.where(map_rows[mp], blk, jnp.zeros_like(blk)))
        return out

    def issue(qw, k, g, slot, masked):
        s = jnp.dot(k, qw[g], preferred_element_type=F32)
        if masked:
            s = jnp.where(causal, s, NEG_BIG)
        s_ref[slot, g] = s
        c_ref[slot, g] = jnp.max(s, axis=0, keepdims=True)

    def reset_stats():
        acc_ref[...] = jnp.zeros_like(acc_ref)
        m_ref[...] = jnp.full_like(m_ref, NEG_BIG)

    def query_tile(qi, carry):
        qw = q_weights(qi)
        qw_next = q_weights(jnp.minimum(qi + 1, n_q - 1))

        def consume(vt, g, slot, mask_now):
            s, c = s_ref[slot, g], c_ref[slot, g]
            if mask_now:
                s = jnp.where(causal, s, NEG_BIG)
                c = jnp.max(s, axis=0, keepdims=True)
            m_old = m_ref[g]
            m_new = jnp.maximum(m_old, c)
            alpha = jnp.exp2(m_old - m_new)
            p = jnp.exp2(s - m_new).astype(BF16)
            acc_ref[g] = alpha * acc_ref[g] + jnp.dot(vt, p, preferred_element_type=F32)
            m_ref[g] = m_new

        def step(qw_nxt, k_next, nxt, vt, cur, slot):
            for idx in range(max(len(nxt), len(cur))):
                if idx < len(nxt):
                    issue(qw_nxt, k_next, nxt[idx][0], 1 - slot, nxt[idx][1])
                if idx < len(cur):
                    consume(vt, cur[idx][0], slot, cur[idx][1])

        n_full = qi * sub

        def body(i, carry):
            for t in range(ATT_UNROLL):
                j = ATT_UNROLL * i + t
                step(qw, key_tile(j + 1), all_groups, vt_ref[0, 0, j], all_groups, t % 2)
            return carry

        lax.fori_loop(0, n_full // ATT_UNROLL, body, 0)

        def diag_groups(jj):
            return [(g, n == jj) for g, (_, n) in enumerate(groups) if n >= jj]

        for jj in range(sub):
            cur = diag_groups(jj) if jj == 0 else [(g, False) for g, _ in diag_groups(jj)]
            vt = vt_ref[0, 0, n_full + jj]
            if jj + 1 < sub:
                step(qw, key_tile(n_full + jj + 1), diag_groups(jj + 1), vt, cur, jj % 2)
            else:
                step(qw_next, key_tile(0), all_groups, vt, cur, jj % 2)
        epilogue(qi)
        reset_stats()
        return carry

    k_first = key_tile(0)
    qw_first = q_weights(0)
    for g, _ in all_groups:
        issue(qw_first, k_first, g, 0, False)
    reset_stats()
    lax.fori_loop(0, n_q, query_tile, 0)


def _diff_attn(qt, k, vt, gate, lq1, lk1, lq2, lk2, subln_g):
    B, H, S, _ = k.shape
    tq, tk, tn = ATT_TQ, ATT_TK, ATT_TN
    assert tk == tn and ATT_UNROLL % 2 == 0 and (tq // tk) % ATT_UNROLL == 0
    n_col_groups = 2 * (tq // tn)
    lam_spec = pl.BlockSpec((1, DA_HEAD_DIM), lambda b, h: (0, 0))
    return pl.pallas_call(
        _diff_attn_kernel,
        grid=(B, H),
        in_specs=[
            pl.BlockSpec((1, 1, S // tn, LANE, tn), lambda b, h: (b, h, 0, 0, 0)),
            pl.BlockSpec((1, 1, S, LANE), lambda b, h: (b, h, 0, 0)),
            pl.BlockSpec((1, 1, S // tk, ATT_VROWS, tk), lambda b, h: (b, h, 0, 0, 0)),
            pl.BlockSpec((1, S, LANE), lambda b, h: (b, 0, h)),
            lam_spec, lam_spec, lam_spec, lam_spec,
            pl.BlockSpec((DA_V_DIM, 1), lambda b, h: (0, 0)),
        ],
        out_specs=pl.BlockSpec((1, S, LANE), lambda b, h: (b, 0, h)),
        out_shape=jax.ShapeDtypeStruct((B, S, DA_WIDTH), BF16),
        scratch_shapes=[
            pltpu.VMEM((n_col_groups, ATT_VROWS, tn), F32),
            pltpu.VMEM((n_col_groups, 1, tn), F32),
            pltpu.VMEM((2, n_col_groups, tk, tn), F32),
            pltpu.VMEM((2, n_col_groups, 1, tn), F32),
        ],
        compiler_params=pltpu.CompilerParams(
            dimension_semantics=("parallel", "parallel"),
            vmem_limit_bytes=VMEM_LIMIT),
        name="diff_attn",
    )(qt, k, vt, gate, lq1.reshape(1, -1), lk1.reshape(1, -1), lq2.reshape(1, -1),
      lk2.reshape(1, -1), subln_g.reshape(DA_V_DIM, 1))


def _gla_kernel(q_ref, k_ref, v_ref, la_ref, gate_ref, g_ref, o_ref, state_ref):
    C = GLA_CHUNK

    @pl.when(pl.program_id(1) == 0)
    def _():
        state_ref[...] = jnp.zeros_like(state_ref)

    r = lax.broadcasted_iota(jnp.int32, (C, C), 0)
    c = lax.broadcasted_iota(jnp.int32, (C, C), 1)
    causal = c <= r
    tri = jnp.where(causal, 1.0, 0.0).astype(BF16)
    tn = (((0,), (0,)), ((), ()))
    nt = (((1,), (1,)), ((), ()))
    chunks = range(GLA_ROWS // C)
    heads = range(GLA_HEADS)
    items = [(ci, h) for ci in chunks for h in heads]
    rows = {ci: slice(ci * C, (ci + 1) * C) for ci in chunks}
    ks = {h: slice(h * GLA_K_DIM, (h + 1) * GLA_K_DIM) for h in heads}
    vs = {h: slice(h * GLA_V_DIM, (h + 1) * GLA_V_DIM) for h in heads}

    b = {}
    for ci, h in items:
        la = la_ref[0, rows[ci], ks[h]]
        la_hi = la.astype(BF16)
        la_lo = (la - la_hi.astype(F32)).astype(BF16)
        b[ci, h] = (jnp.dot(tri, la_hi, preferred_element_type=F32)
                    + jnp.dot(tri, la_lo, preferred_element_type=F32))
    v = {(ci, h): v_ref[0, rows[ci], vs[h]] for ci, h in items}
    q_dec, k_inv, k_tail, decay = {}, {}, {}, {}
    for ci, h in items:
        bb = b[ci, h]
        b_last = bb[C - 1:C, :]
        q = q_ref[0, rows[ci], ks[h]].astype(F32)
        k = k_ref[0, rows[ci], ks[h]].astype(F32)
        q_dec[ci, h] = (q * jnp.exp2(bb)).astype(BF16)
        k_inv[ci, h] = (k * jnp.exp2(-bb)).astype(BF16)
        k_tail[ci, h] = (k * jnp.exp2(b_last - bb)).astype(BF16)
        decay[ci, h] = jnp.exp2(b_last)
    attn = {i: jnp.where(causal, lax.dot_general(q_dec[i], k_inv[i], nt,
                                                 preferred_element_type=F32), 0.0).astype(BF16)
            for i in items}
    kv_t = {i: lax.dot_general(v[i], k_tail[i], tn, preferred_element_type=F32)
            for i in items}
    intra = {i: jnp.dot(attn[i], v[i], preferred_element_type=F32) for i in items}

    for ci in chunks:
        o = {}
        for h in heads:
            state_t = state_ref[h]
            o[h] = intra[ci, h] + lax.dot_general(q_dec[ci, h], state_t.astype(BF16), nt,
                                                  preferred_element_type=F32)
            state_ref[h] = state_t * decay[ci, h] + kv_t[ci, h]
        for h in heads:
            ms = jnp.mean(o[h] * o[h], axis=-1, keepdims=True)
            y = o[h] * lax.rsqrt(ms + NORM_EPS) * g_ref[...]
            o_ref[0, rows[ci], vs[h]] = (y * gate_ref[0, rows[ci], vs[h]].astype(F32)).astype(BF16)


def _gla(qb, kb, vb, la, gate, gla_g):
    B, S, _ = qb.shape
    tg = GLA_ROWS
    row3 = lambda width: pl.BlockSpec((1, tg, width), lambda b, i: (b, i, 0))
    return pl.pallas_call(
        _gla_kernel,
        grid=(B, S // tg),
        in_specs=[row3(GLA_QK_WIDTH), row3(GLA_QK_WIDTH), row3(GLA_WIDTH),
                  row3(GLA_QK_WIDTH), row3(GLA_WIDTH),
                  pl.BlockSpec((1, GLA_V_DIM), lambda b, i: (0, 0))],
        out_specs=row3(GLA_WIDTH),
        out_shape=jax.ShapeDtypeStruct((B, S, GLA_WIDTH), BF16),
        scratch_shapes=[pltpu.VMEM((GLA_HEADS, GLA_V_DIM, GLA_K_DIM), F32)],
        compiler_params=pltpu.CompilerParams(
            dimension_semantics=("parallel", "arbitrary"),
            vmem_limit_bytes=VMEM_LIMIT),
        name="gla",
    )(qb, kb, vb, la, gate, gla_g.reshape(1, GLA_V_DIM))


def _out_proj_kernel(oa_ref, ob_ref, w_ref, x_ref, g_ref, o_ref):
    y = (jnp.dot(oa_ref[0], w_ref[:DA_WIDTH, :], preferred_element_type=F32)
         + jnp.dot(ob_ref[0], w_ref[DA_WIDTH:, :], preferred_element_type=F32))
    ms = jnp.mean(y * y, axis=-1, keepdims=True)
    o_ref[0] = x_ref[0] + y * lax.rsqrt(ms + NORM_EPS) * g_ref[...]


def _out_proj(oa, ob, w_out, x, post_g):
    B, S, D = x.shape
    tm = OUT_ROWS
    row3 = lambda width: pl.BlockSpec((1, tm, width), lambda b, i: (b, i, 0))
    const = dict(pipeline_mode=pl.Buffered(1))
    return pl.pallas_call(
        _out_proj_kernel,
        grid=(B, S // tm),
        in_specs=[row3(DA_WIDTH), row3(GLA_WIDTH),
                  pl.BlockSpec(w_out.shape, lambda b, i: (0, 0), **const),
                  row3(D),
                  pl.BlockSpec((1, D), lambda b, i: (0, 0), **const)],
        out_specs=row3(D),
        out_shape=jax.ShapeDtypeStruct((B, S, D), F32),
        compiler_params=pltpu.CompilerParams(
            dimension_semantics=("parallel", "parallel"),
            vmem_limit_bytes=VMEM_LIMIT),
        name="out_proj",
    )(oa, ob, w_out.astype(BF16), x, post_g.reshape(1, D))


def kernel(x, pre_norm_g, post_norm_g, w_in, w_gk_up, b_gk, lambda_q1, lambda_k1,
           lambda_q2, lambda_k2, attn_subln_g, gla_norm_g, w_out):
    assert x.shape[-1] == D_MODEL and pre_norm_g.shape[0] == 1
    qt, vt, k, ga, qb, kb, vb, gb, la = _in_proj(
        x, pre_norm_g[0], w_in[0], w_gk_up[0], b_gk[0])
    oa = _diff_attn(qt, k, vt, ga, lambda_q1[0], lambda_k1[0], lambda_q2[0],
                    lambda_k2[0], attn_subln_g[0])
    ob = _gla(qb, kb, vb, la, gb, gla_norm_g[0])
    return _out_proj(oa, ob, w_out[0], x, post_norm_g[0])
```

```python
import math

import jax
import jax.numpy as jnp
from jax import lax
from jax.experimental import pallas as pl
from jax.experimental.pallas import tpu as pltpu

D_MODEL = 1024
DA_HEADS = 8
DA_HEAD_DIM = 64
DA_V_DIM = 128
DA_WIDTH = DA_HEADS * DA_V_DIM
GLA_HEADS = 4
GLA_K_DIM = 128
GLA_V_DIM = 256
GLA_QK_WIDTH = GLA_HEADS * GLA_K_DIM
GLA_WIDTH = GLA_HEADS * GLA_V_DIM
GLA_GATE_RANK = 16
GLA_GATE_NORMALIZER = 16.0
GLA_CHUNK = 64
NORM_EPS = 1e-6
LAM_INIT = 0.8 - 0.6 * math.exp(-0.3 * 0)

LANE = 128
BF16_SUBLANES = 16
VMEM_LIMIT = 56 * 1024 * 1024

PROJ_ROWS = 512
ATT_TQ = 4096
ATT_TK = 256
ATT_VROWS = DA_V_DIM + BF16_SUBLANES
ATT_TN = 256
ATT_UNROLL = 2
GLA_ROWS = 512
OUT_ROWS = 1024
NEG_BIG = -1e30

BF16 = jnp.bfloat16
F32 = jnp.float32


def _silu(x):
    return x * (1.0 / (1.0 + jnp.exp(-x)))


def _log_sigmoid(x):
    return jnp.minimum(x, 0.0) - jnp.log(1.0 + jnp.exp(-jnp.abs(x)))


def _in_proj_kernel(x_ref, g_ref, wt_ref, w_ref, wgk_ref, bgk_ref,
                    qt_ref, vt_ref, k_ref, ga_ref, qb_ref, kb_ref, vb_ref,
                    gb_ref, la_ref):
    x = x_ref[0]
    ms = jnp.mean(x * x, axis=-1, keepdims=True)
    h = (x * lax.rsqrt(ms + NORM_EPS) * g_ref[...]).astype(BF16)

    nt = (((1,), (1,)), ((), ()))
    zt = lax.dot_general(wt_ref[...], h, nt, preferred_element_type=F32)
    q_scale = DA_HEAD_DIM ** -0.5 * math.log2(math.e)
    ones = jnp.ones((BF16_SUBLANES, ATT_TK), BF16)
    for hh in range(DA_HEADS):
        lo = hh * LANE
        for blk in range(x.shape[0] // ATT_TK):
            cs = slice(blk * ATT_TK, (blk + 1) * ATT_TK)
            qt_ref[0, hh, blk] = (zt[lo:lo + LANE, cs] * q_scale).astype(BF16)
            vt_ref[0, hh, blk, :DA_V_DIM, :] = zt[DA_WIDTH + lo:DA_WIDTH + lo + LANE, cs].astype(BF16)
            vt_ref[0, hh, blk, DA_V_DIM:, :] = ones

    def proj(c0, width):
        return jnp.dot(h, w_ref[:, c0:c0 + width], preferred_element_type=F32)

    zk = proj(0, 1024)
    for hh in range(DA_HEADS):
        k_ref[0, hh] = zk[:, hh * LANE:(hh + 1) * LANE].astype(BF16)
    ga_ref[0] = _silu(proj(1024, 1024)).astype(BF16)
    zqk = proj(2048, 1024)
    qb_ref[0] = (zqk[:, :GLA_QK_WIDTH] * (GLA_K_DIM ** -0.5)).astype(BF16)
    kb_ref[0] = zqk[:, GLA_QK_WIDTH:].astype(BF16)
    vb_ref[0] = proj(3072, 1024).astype(BF16)
    gb_ref[0] = _silu(proj(4096, 1024)).astype(BF16)
    gk_low = proj(5120, LANE).astype(BF16)
    gk = jnp.dot(gk_low, wgk_ref[...], preferred_element_type=F32) + bgk_ref[...]
    la_ref[0] = _log_sigmoid(gk) * (math.log2(math.e) / GLA_GATE_NORMALIZER)


def _in_proj(x, pre_g, w_in, w_gk_up, b_gk):
    B, S, D = x.shape
    tm = PROJ_ROWS
    nblk = tm // ATT_TK
    c = 0
    cols = {}
    for name, width in (("qa", 1024), ("ka", 1024), ("va", 1024), ("ga", 1024),
                        ("qb", 512), ("kb", 512), ("vb", 1024), ("gb", 1024),
                        ("gk", GLA_GATE_RANK)):
        cols[name] = (c, c + width)
        c += width

    def sl(name):
        a, b = cols[name]
        return w_in[:, a:b]

    wt = jnp.concatenate([sl("qa"), sl("va")], axis=1).T.astype(BF16)
    w = jnp.concatenate(
        [sl("ka"), sl("ga"), sl("qb"), sl("kb"), sl("vb"), sl("gb"),
         jnp.pad(sl("gk"), ((0, 0), (0, LANE - GLA_GATE_RANK)))], axis=1).astype(BF16)
    wgk = jnp.pad(w_gk_up, ((0, LANE - GLA_GATE_RANK), (0, 0))).astype(BF16)
    bgk = b_gk.reshape(1, GLA_QK_WIDTH)
    g = pre_g.reshape(1, D)

    const = dict(pipeline_mode=pl.Buffered(1))
    in_specs = [
        pl.BlockSpec((1, tm, D), lambda b, i: (b, i, 0)),
        pl.BlockSpec((1, D), lambda b, i: (0, 0), **const),
        pl.BlockSpec(wt.shape, lambda b, i: (0, 0), **const),
        pl.BlockSpec(w.shape, lambda b, i: (0, 0), **const),
        pl.BlockSpec(wgk.shape, lambda b, i: (0, 0), **const),
        pl.BlockSpec(bgk.shape, lambda b, i: (0, 0), **const),
    ]
    row3 = lambda width: pl.BlockSpec((1, tm, width), lambda b, i: (b, i, 0))
    out_specs = [
        pl.BlockSpec((1, DA_HEADS, nblk, LANE, ATT_TK), lambda b, i: (b, 0, i, 0, 0)),
        pl.BlockSpec((1, DA_HEADS, nblk, ATT_VROWS, ATT_TK), lambda b, i: (b, 0, i, 0, 0)),
        pl.BlockSpec((1, DA_HEADS, tm, LANE), lambda b, i: (b, 0, i, 0)),
        row3(DA_WIDTH),
        row3(GLA_QK_WIDTH), row3(GLA_QK_WIDTH), row3(GLA_WIDTH), row3(GLA_WIDTH),
        row3(GLA_QK_WIDTH),
    ]
    out_shape = [
        jax.ShapeDtypeStruct((B, DA_HEADS, S // ATT_TK, LANE, ATT_TK), BF16),
        jax.ShapeDtypeStruct((B, DA_HEADS, S // ATT_TK, ATT_VROWS, ATT_TK), BF16),
        jax.ShapeDtypeStruct((B, DA_HEADS, S, LANE), BF16),
        jax.ShapeDtypeStruct((B, S, DA_WIDTH), BF16),
        jax.ShapeDtypeStruct((B, S, GLA_QK_WIDTH), BF16),
        jax.ShapeDtypeStruct((B, S, GLA_QK_WIDTH), BF16),
        jax.ShapeDtypeStruct((B, S, GLA_WIDTH), BF16),
        jax.ShapeDtypeStruct((B, S, GLA_WIDTH), BF16),
        jax.ShapeDtypeStruct((B, S, GLA_QK_WIDTH), F32),
    ]
    return pl.pallas_call(
        _in_proj_kernel,
        grid=(B, S // tm),
        in_specs=in_specs,
        out_specs=out_specs,
        out_shape=out_shape,
        compiler_params=pltpu.CompilerParams(
            dimension_semantics=("parallel", "parallel"),
            vmem_limit_bytes=VMEM_LIMIT),
        name="in_proj",
    )(x, g, wt, w, wgk, bgk)


def _diff_attn_kernel(qt_ref, k_ref, vt_ref, gate_ref, lq1_ref, lk1_ref, lq2_ref,
                      lk2_ref, g_ref, o_ref, acc_ref, m_ref, s_ref, c_ref):
    tq, tk, tn = ATT_TQ, ATT_TK, ATT_TN
    n_tiles = tq // tn
    sub = tq // tk
    n_q = o_ref.shape[1] // tq
    groups = [(mp, n) for mp in range(2) for n in range(n_tiles)]
    all_groups = [(g, False) for g in range(len(groups))]
    row = lax.broadcasted_iota(jnp.int32, (LANE, tn), 0)
    map_rows = (row < DA_HEAD_DIM, row >= DA_HEAD_DIM)
    causal = (lax.broadcasted_iota(jnp.int32, (tk, tn), 0)
              <= lax.broadcasted_iota(jnp.int32, (tk, tn), 1))
    lam = (jnp.exp(jnp.sum(lq1_ref[...] * lk1_ref[...], axis=-1, keepdims=True))
           - jnp.exp(jnp.sum(lq2_ref[...] * lk2_ref[...], axis=-1, keepdims=True))
           + LAM_INIT)
    gain = jnp.broadcast_to(g_ref[...] * (1.0 - LAM_INIT), (DA_V_DIM, tn))

    def aligned(start, size):
        return pl.ds(start if isinstance(start, int) else pl.multiple_of(start, size), size)

    def key_tile(j):
        return k_ref[0, 0, aligned(j * tk, tk), :]

    def epilogue(qi):
        for n in range(n_tiles):
            a1, a2 = acc_ref[n], acc_ref[n_tiles + n]
            l1, l2 = a1[DA_V_DIM:DA_V_DIM + 1], a2[DA_V_DIM:DA_V_DIM + 1]
            o = a1[:DA_V_DIM] * (1.0 / l1) - a2[:DA_V_DIM] * (lam / l2)
            msq = jnp.mean(o * o, axis=0, keepdims=True)
            y = o * lax.rsqrt(msq + NORM_EPS) * gain
            rows = aligned(qi * tq + n * tn, tn)
            o_ref[0, rows, :] = (y.T * gate_ref[0, rows, :].astype(F32)).astype(BF16)

    def q_weights(qi):
        out = []
        for mp, n in groups:
            blk = qt_ref[0, 0, qi * n_tiles + n]
            out.append(jnp.where(map_rows[mp], blk, jnp.zeros_like(blk)))
        return out

    def issue(qw, k, g, slot, masked):
        s = jnp.dot(k, qw[g], preferred_element_type=F32)
        if masked:
            s = jnp.where(causal, s, NEG_BIG)
        s_ref[slot, g] = s
        c_ref[slot, g] = jnp.max(s, axis=0, keepdims=True)

    def reset_stats():
        acc_ref[...] = jnp.zeros_like(acc_ref)
        m_ref[...] = jnp.full_like(m_ref, NEG_BIG)

    def query_tile(qi, carry):
        qw = q_weights(qi)
        qw_next = q_weights(jnp.minimum(qi + 1, n_q - 1))

        def consume(vt, g, slot, mask_now):
            s, c = s_ref[slot, g], c_ref[slot, g]
            if mask_now:
                s = jnp.where(causal, s, NEG_BIG)
                c = jnp.max(s, axis=0, keepdims=True)
            m_old = m_ref[g]
            m_new = jnp.maximum(m_old, c)
            alpha = jnp.exp2(m_old - m_new)
            p = jnp.exp2(s - m_new).astype(BF16)
            acc_ref[g] = alpha * acc_ref[g] + jnp.dot(vt, p, preferred_element_type=F32)
            m_ref[g] = m_new

        def step(qw_nxt, k_next, nxt, vt, cur, slot):
            for idx in range(max(len(nxt), len(cur))):
                if idx < len(nxt):
                    issue(qw_nxt, k_next, nxt[idx][0], 1 - slot, nxt[idx][1])
                if idx < len(cur):
                    consume(vt, cur[idx][0], slot, cur[idx][1])

        n_full = qi * sub

        def body(i, carry):
            for t in range(ATT_UNROLL):
                j = ATT_UNROLL * i + t
                step(qw, key_tile(j + 1), all_groups, vt_ref[0, 0, j], all_groups, t % 2)
            return carry

        lax.fori_loop(0, n_full // ATT_UNROLL, body, 0)

        def diag_groups(jj):
            return [(g, n == jj) for g, (_, n) in enumerate(groups) if n >= jj]

        for jj in range(sub):
            cur = diag_groups(jj) if jj == 0 else [(g, False) for g, _ in diag_groups(jj)]
            vt = vt_ref[0, 0, n_full + jj]
            if jj + 1 < sub:
                step(qw, key_tile(n_full + jj + 1), diag_groups(jj + 1), vt, cur, jj % 2)
            else:
                step(qw_next, key_tile(0), all_groups, vt, cur, jj % 2)
        epilogue(qi)
        reset_stats()
        return carry

    k_first = key_tile(0)
    qw_first = q_weights(0)
    for g, _ in all_groups:
        issue(qw_first, k_first, g, 0, False)
    reset_stats()
    lax.fori_loop(0, n_q, query_tile, 0)


def _diff_attn(qt, k, vt, gate, lq1, lk1, lq2, lk2, subln_g):
    B, H, S, _ = k.shape
    tq, tk, tn = ATT_TQ, ATT_TK, ATT_TN
    assert tk == tn and ATT_UNROLL % 2 == 0 and (tq // tk) % ATT_UNROLL == 0
    n_col_groups = 2 * (tq // tn)
    lam_spec = pl.BlockSpec((1, DA_HEAD_DIM), lambda b, h: (0, 0))
    return pl.pallas_call(
        _diff_attn_kernel,
        grid=(B, H),
        in_specs=[
            pl.BlockSpec((1, 1, S // tn, LANE, tn), lambda b, h: (b, h, 0, 0, 0)),
            pl.BlockSpec((1, 1, S, LANE), lambda b, h: (b, h, 0, 0)),
            pl.BlockSpec((1, 1, S // tk, ATT_VROWS, tk), lambda b, h: (b, h, 0, 0, 0)),
            pl.BlockSpec((1, S, LANE), lambda b, h: (b, 0, h)),
            lam_spec, lam_spec, lam_spec, lam_spec,
            pl.BlockSpec((DA_V_DIM, 1), lambda b, h: (0, 0)),
        ],
        out_specs=pl.BlockSpec((1, S, LANE), lambda b, h: (b, 0, h)),
        out_shape=jax.ShapeDtypeStruct((B, S, DA_WIDTH), BF16),
        scratch_shapes=[
            pltpu.VMEM((n_col_groups, ATT_VROWS, tn), F32),
            pltpu.VMEM((n_col_groups, 1, tn), F32),
            pltpu.VMEM((2, n_col_groups, tk, tn), F32),
            pltpu.VMEM((2, n_col_groups, 1, tn), F32),
        ],
        compiler_params=pltpu.CompilerParams(
            dimension_semantics=("parallel", "parallel"),
            vmem_limit_bytes=VMEM_LIMIT),
        name="diff_attn",
    )(qt, k, vt, gate, lq1.reshape(1, -1), lk1.reshape(1, -1), lq2.reshape(1, -1),
      lk2.reshape(1, -1), subln_g.reshape(DA_V_DIM, 1))


def _gla_kernel(q_ref, k_ref, v_ref, la_ref, gate_ref, g_ref, o_ref, state_ref):
    C = GLA_CHUNK

    @pl.when(pl.program_id(1) == 0)
    def _():
        state_ref[...] = jnp.zeros_like(state_ref)

    r = lax.broadcasted_iota(jnp.int32, (C, C), 0)
    c = lax.broadcasted_iota(jnp.int32, (C, C), 1)
    causal = c <= r
    tri = jnp.where(causal, 1.0, 0.0).astype(BF16)
    tn = (((0,), (0,)), ((), ()))
    nt = (((1,), (1,)), ((), ()))
    chunks = range(GLA_ROWS // C)
    heads = range(GLA_HEADS)
    items = [(ci, h) for ci in chunks for h in heads]
    rows = {ci: slice(ci * C, (ci + 1) * C) for ci in chunks}
    ks = {h: slice(h * GLA_K_DIM, (h + 1) * GLA_K_DIM) for h in heads}
    vs = {h: slice(h * GLA_V_DIM, (h + 1) * GLA_V_DIM) for h in heads}

    b = {}
    for ci, h in items:
        la = la_ref[0, rows[ci], ks[h]]
        la_hi = la.astype(BF16)
        la_lo = (la - la_hi.astype(F32)).astype(BF16)
        b[ci, h] = (jnp.dot(tri, la_hi, preferred_element_type=F32)
                    + jnp.dot(tri, la_lo, preferred_element_type=F32))
    v = {(ci, h): v_ref[0, rows[ci], vs[h]] for ci, h in items}
    q_dec, k_inv, k_tail, decay = {}, {}, {}, {}
    for ci, h in items:
        bb = b[ci, h]
        b_last = bb[C - 1:C, :]
        q = q_ref[0, rows[ci], ks[h]].astype(F32)
        k = k_ref[0, rows[ci], ks[h]].astype(F32)
        q_dec[ci, h] = (q * jnp.exp2(bb)).astype(BF16)
        k_inv[ci, h] = (k * jnp.exp2(-bb)).astype(BF16)
        k_tail[ci, h] = (k * jnp.exp2(b_last - bb)).astype(BF16)
        decay[ci, h] = jnp.exp2(b_last)
    attn = {i: jnp.where(causal, lax.dot_general(q_dec[i], k_inv[i], nt,
                                                 preferred_element_type=F32), 0.0).astype(BF16)
            for i in items}
    kv_t = {i: lax.dot_general(v[i], k_tail[i], tn, preferred_element_type=F32)
            for i in items}
    intra = {i: jnp.dot(attn[i], v[i], preferred_element_type=F32) for i in items}

    for ci in chunks:
        o = {}
        for h in heads:
            state_t = state_ref[h]
            o[h] = intra[ci, h] + lax.dot_general(q_dec[ci, h], state_t.astype(BF16), nt,
                                                  preferred_element_type=F32)
            state_ref[h] = state_t * decay[ci, h] + kv_t[ci, h]
        for h in heads:
            ms = jnp.mean(o[h] * o[h], axis=-1, keepdims=True)
            y = o[h] * lax.rsqrt(ms + NORM_EPS) * g_ref[...]
            o_ref[0, rows[ci], vs[h]] = (y * gate_ref[0, rows[ci], vs[h]].astype(F32)).astype(BF16)


def _gla(qb, kb, vb, la, gate, gla_g):
    B, S, _ = qb.shape
    tg = GLA_ROWS
    row3 = lambda width: pl.BlockSpec((1, tg, width), lambda b, i: (b, i, 0))
    return pl.pallas_call(
        _gla_kernel,
        grid=(B, S // tg),
        in_specs=[row3(GLA_QK_WIDTH), row3(GLA_QK_WIDTH), row3(GLA_WIDTH),
                  row3(GLA_QK_WIDTH), row3(GLA_WIDTH),
                  pl.BlockSpec((1, GLA_V_DIM), lambda b, i: (0, 0))],
        out_specs=row3(GLA_WIDTH),
        out_shape=jax.ShapeDtypeStruct((B, S, GLA_WIDTH), BF16),
        scratch_shapes=[pltpu.VMEM((GLA_HEADS, GLA_V_DIM, GLA_K_DIM), F32)],
        compiler_params=pltpu.CompilerParams(
            dimension_semantics=("parallel", "arbitrary"),
            vmem_limit_bytes=VMEM_LIMIT),
        name="gla",
    )(qb, kb, vb, la, gate, gla_g.reshape(1, GLA_V_DIM))


def _out_proj_kernel(oa_ref, ob_ref, w_ref, x_ref, g_ref, o_ref):
    y = (jnp.dot(oa_ref[0], w_ref[:DA_WIDTH, :], preferred_element_type=F32)
         + jnp.dot(ob_ref[0], w_ref[DA_WIDTH:, :], preferred_element_type=F32))
    ms = jnp.mean(y * y, axis=-1, keepdims=True)
    o_ref[0] = x_ref[0] + y * lax.rsqrt(ms + NORM_EPS) * g_ref[...]


def _out_proj(oa, ob, w_out, x, post_g):
    B, S, D = x.shape
    tm = OUT_ROWS
    row3 = lambda width: pl.BlockSpec((1, tm, width), lambda b, i: (b, i, 0))
    const = dict(pipeline_mode=pl.Buffered(1))
    return pl.pallas_call(
        _out_proj_kernel,
        grid=(B, S // tm),
        in_specs=[row3(DA_WIDTH), row3(GLA_WIDTH),
                  pl.BlockSpec(w_out.shape, lambda b, i: (0, 0), **const),
                  row3(D),
                  pl.BlockSpec((1, D), lambda b, i: (0, 0), **const)],
        out_specs=row3(D),
        out_shape=jax.ShapeDtypeStruct((B, S, D), F32),
        compiler_params=pltpu.CompilerParams(
            dimension_semantics=("parallel", "parallel"),
            vmem_limit_bytes=VMEM_LIMIT),
        name="out_proj",
    )(oa, ob, w_out.astype(BF16), x, post_g.reshape(1, D))


def kernel(x, pre_norm_g, post_norm_g, w_in, w_gk_up, b_gk, lambda_q1, lambda_k1,
           lambda_q2, lambda_k2, attn_subln_g, gla_norm_g, w_out):
    assert x.shape[-1] == D_MODEL and pre_norm_g.shape[0] == 1
    qt, vt, k, ga, qb, kb, vb, gb, la = _in_proj(
        x, pre_norm_g[0], w_in[0], w_gk_up[0], b_gk[0])
    oa = _diff_attn(qt, k, vt, ga, lambda_q1[0], lambda_k1[0], lambda_q2[0],
                    lambda_k2[0], attn_subln_g[0])
    ob = _gla(qb, kb, vb, la, gb, gla_norm_g[0])
    return _out_proj(oa, ob, w_out[0], x, post_norm_g[0])
```

```python
import math

import jax
import jax.numpy as jnp
from jax import lax
from jax.experimental import pallas as pl
from jax.experimental.pallas import tpu as pltpu

D_MODEL = 1024
DA_HEADS = 8
DA_HEAD_DIM = 64
DA_V_DIM = 128
DA_WIDTH = DA_HEADS * DA_V_DIM
GLA_HEADS = 4
GLA_K_DIM = 128
GLA_V_DIM = 256
GLA_QK_WIDTH = GLA_HEADS * GLA_K_DIM
GLA_WIDTH = GLA_HEADS * GLA_V_DIM
GLA_GATE_RANK = 16
GLA_GATE_NORMALIZER = 16.0
GLA_CHUNK = 64
NORM_EPS = 1e-6
LAM_INIT = 0.8 - 0.6 * math.exp(-0.3 * 0)

LANE = 128
BF16_SUBLANES = 16
VMEM_LIMIT = 56 * 1024 * 1024

PROJ_ROWS = 512
ATT_TQ = 4096
ATT_TK = 256
ATT_VROWS = DA_V_DIM + BF16_SUBLANES
ATT_TN = 256
ATT_UNROLL = 2
GLA_ROWS = 512
OUT_ROWS = 1024
NEG_BIG = -1e30

BF16 = jnp.bfloat16
F32 = jnp.float32


def _silu(x):
    return x * (1.0 / (1.0 + jnp.exp(-x)))


def _log_sigmoid(x):
    return jnp.minimum(x, 0.0) - jnp.log(1.0 + jnp.exp(-jnp.abs(x)))


def _in_proj_kernel(x_ref, g_ref, wt_ref, w_ref, wgk_ref, bgk_ref,
                    qt_ref, vt_ref, k_ref, ga_ref, qb_ref, kb_ref, vb_ref,
                    gb_ref, la_ref):
    x = x_ref[0]
    ms = jnp.mean(x * x, axis=-1, keepdims=True)
    h = (x * lax.rsqrt(ms + NORM_EPS) * g_ref[...]).astype(BF16)

    nt = (((1,), (1,)), ((), ()))
    zt = lax.dot_general(wt_ref[...], h, nt, preferred_element_type=F32)
    q_scale = DA_HEAD_DIM ** -0.5 * math.log2(math.e)
    ones = jnp.ones((BF16_SUBLANES, ATT_TK), BF16)
    for hh in range(DA_HEADS):
        lo = hh * LANE
        for blk in range(x.shape[0] // ATT_TK):
            cs = slice(blk * ATT_TK, (blk + 1) * ATT_TK)
            qt_ref[0, hh, blk] = (zt[lo:lo + LANE, cs] * q_scale).astype(BF16)
            vt_ref[0, hh, blk, :DA_V_DIM, :] = zt[DA_WIDTH + lo:DA_WIDTH + lo + LANE, cs].astype(BF16)
            vt_ref[0, hh, blk, DA_V_DIM:, :] = ones

    def proj(c0, width):
        return jnp.dot(h, w_ref[:, c0:c0 + width], preferred_element_type=F32)

    zk = proj(0, 1024)
    for hh in range(DA_HEADS):
        k_ref[0, hh] = zk[:, hh * LANE:(hh + 1) * LANE].astype(BF16)
    ga_ref[0] = _silu(proj(1024, 1024)).astype(BF16)
    zqk = proj(2048, 1024)
    qb_ref[0] = (zqk[:, :GLA_QK_WIDTH] * (GLA_K_DIM ** -0.5)).astype(BF16)
    kb_ref[0] = zqk[:, GLA_QK_WIDTH:].astype(BF16)
    vb_ref[0] = proj(3072, 1024).astype(BF16)
    gb_ref[0] = _silu(proj(4096, 1024)).astype(BF16)
    gk_low = proj(5120, LANE).astype(BF16)
    gk = jnp.dot(gk_low, wgk_ref[...], preferred_element_type=F32) + bgk_ref[...]
    la_ref[0] = _log_sigmoid(gk) * (math.log2(math.e) / GLA_GATE_NORMALIZER)


def _in_proj(x, pre_g, w_in, w_gk_up, b_gk):
    B, S, D = x.shape
    tm = PROJ_ROWS
    nblk = tm // ATT_TK
    c = 0
    cols = {}
    for name, width in (("qa", 1024), ("ka", 1024), ("va", 1024), ("ga", 1024),
                        ("qb", 512), ("kb", 512), ("vb", 1024), ("gb", 1024),
                        ("gk", GLA_GATE_RANK)):
        cols[name] = (c, c + width)
        c += width

    def sl(name):
        a, b = cols[name]
        return w_in[:, a:b]

    wt = jnp.concatenate([sl("qa"), sl("va")], axis=1).T.astype(BF16)
    w = jnp.concatenate(
        [sl("ka"), sl("ga"), sl("qb"), sl("kb"), sl("vb"), sl("gb"),
         jnp.pad(sl("gk"), ((0, 0), (0, LANE - GLA_GATE_RANK)))], axis=1).astype(BF16)
    wgk = jnp.pad(w_gk_up, ((0, LANE - GLA_GATE_RANK), (0, 0))).astype(BF16)
    bgk = b_gk.reshape(1, GLA_QK_WIDTH)
    g = pre_g.reshape(1, D)

    const = dict(pipeline_mode=pl.Buffered(1))
    in_specs = [
        pl.BlockSpec((1, tm, D), lambda b, i: (b, i, 0)),
        pl.BlockSpec((1, D), lambda b, i: (0, 0), **const),
        pl.BlockSpec(wt.shape, lambda b, i: (0, 0), **const),
        pl.BlockSpec(w.shape, lambda b, i: (0, 0), **const),
        pl.BlockSpec(wgk.shape, lambda b, i: (0, 0), **const),
        pl.BlockSpec(bgk.shape, lambda b, i: (0, 0), **const),
    ]
    row3 = lambda width: pl.BlockSpec((1, tm, width), lambda b, i: (b, i, 0))
    out_specs = [
        pl.BlockSpec((1, DA_HEADS, nblk, LANE, ATT_TK), lambda b, i: (b, 0, i, 0, 0)),
        pl.BlockSpec((1, DA_HEADS, nblk, ATT_VROWS, ATT_TK), lambda b, i: (b, 0, i, 0, 0)),
        pl.BlockSpec((1, DA_HEADS, tm, LANE), lambda b, i: (b, 0, i, 0)),
        row3(DA_WIDTH),
        row3(GLA_QK_WIDTH), row3(GLA_QK_WIDTH), row3(GLA_WIDTH), row3(GLA_WIDTH),
        row3(GLA_QK_WIDTH),
    ]
    out_shape = [
        jax.ShapeDtypeStruct((B, DA_HEADS, S // ATT_TK, LANE, ATT_TK), BF16),
        jax.ShapeDtypeStruct((B, DA_HEADS, S // ATT_TK, ATT_VROWS, ATT_TK), BF16),
        jax.ShapeDtypeStruct((B, DA_HEADS, S, LANE), BF16),
        jax.ShapeDtypeStruct((B, S, DA_WIDTH), BF16),
        jax.ShapeDtypeStruct((B, S, GLA_QK_WIDTH), BF16),
        jax.ShapeDtypeStruct((B, S, GLA_QK_WIDTH), BF16),
        jax.ShapeDtypeStruct((B, S, GLA_WIDTH), BF16),
        jax.ShapeDtypeStruct((B, S, GLA_WIDTH), BF16),
        jax.ShapeDtypeStruct((B, S, GLA_QK_WIDTH), F32),
    ]
    return pl.pallas_call(
        _in_proj_kernel,
        grid=(B, S // tm),
        in_specs=in_specs,
        out_specs=out_specs,
        out_shape=out_shape,
        compiler_params=pltpu.CompilerParams(
            dimension_semantics=("parallel", "parallel"),
            vmem_limit_bytes=VMEM_LIMIT),
        name="in_proj",
    )(x, g, wt, w, wgk, bgk)


def _diff_attn_kernel(qt_ref, k_ref, vt_ref, gate_ref, lq1_ref, lk1_ref, lq2_ref,
                      lk2_ref, g_ref, o_ref, acc_ref, m_ref, s_ref, c_ref):
    tq, tk, tn = ATT_TQ, ATT_TK, ATT_TN
    n_tiles = tq // tn
    sub = tq // tk
    n_q = o_ref.shape[1] // tq
    groups = [(mp, n) for mp in range(2) for n in range(n_tiles)]
    all_groups = [(g, False) for g in range(len(groups))]
    row = lax.broadcasted_iota(jnp.int32, (LANE, tn), 0)
    map_rows = (row < DA_HEAD_DIM, row >= DA_HEAD_DIM)
    causal = (lax.broadcasted_iota(jnp.int32, (tk, tn), 0)
              <= lax.broadcasted_iota(jnp.int32, (tk, tn), 1))
    lam = (jnp.exp(jnp.sum(lq1_ref[...] * lk1_ref[...], axis=-1, keepdims=True))
           - jnp.exp(jnp.sum(lq2_ref[...] * lk2_ref[...], axis=-1, keepdims=True))
           + LAM_INIT)
    gain = jnp.broadcast_to(g_ref[...] * (1.0 - LAM_INIT), (DA_V_DIM, tn))

    def aligned(start, size):
        return pl.ds(start if isinstance(start, int) else pl.multiple_of(start, size), size)

    def key_tile(j):
        return k_ref[0, 0, aligned(j * tk, tk), :]

    def epilogue(qi):
        for n in range(n_tiles):
            a1, a2 = acc_ref[n], acc_ref[n_tiles + n]
            l1, l2 = a1[DA_V_DIM:DA_V_DIM + 1], a2[DA_V_DIM:DA_V_DIM + 1]
            o = a1[:DA_V_DIM] * (1.0 / l1) - a2[:DA_V_DIM] * (lam / l2)
            msq = jnp.mean(o * o, axis=0, keepdims=True)
            y = o * lax.rsqrt(msq + NORM_EPS) * gain
            rows = aligned(qi * tq + n * tn, tn)
            o_ref[0, rows, :] = (y.T * gate_ref[0, rows, :].astype(F32)).astype(BF16)

    def q_weights(qi):
        out = []
        for mp, n in groups:
            blk = qt_ref[0, 0, qi * n_tiles + n]
            out.append(jnp.where(map_rows[mp], blk, jnp.zeros_like(blk)))
        return out

    def issue(qw, k, g, slot, masked):
        s = jnp.dot(k, qw[g], preferred_element_type=F32)
        if masked:
            s = jnp.where(causal, s, NEG_BIG)
        s_ref[slot, g] = s
        c_ref[slot, g] = jnp.max(s, axis=0, keepdims=True)

    def reset_stats():
        acc_ref[...] = jnp.zeros_like(acc_ref)
        m_ref[...] = jnp.full_like(m_ref, NEG_BIG)

    def query_tile(qi):
        last = qi == n_q - 1
        qw = q_weights(qi)
        qw_next = None if last else q_weights(qi + 1)

        def consume(vt, g, slot, mask_now):
            s, c = s_ref[slot, g], c_ref[slot, g]
            if mask_now:
                s = jnp.where(causal, s, NEG_BIG)
                c = jnp.max(s, axis=0, keepdims=True)
            m_old = m_ref[g]
            m_new = jnp.maximum(m_old, c)
            alpha = jnp.exp2(m_old - m_new)
            p = jnp.exp2(s - m_new).astype(BF16)
            acc_ref[g] = alpha * acc_ref[g] + jnp.dot(vt, p, preferred_element_type=F32)
            m_ref[g] = m_new

        def step(qw_nxt, k_next, nxt, vt, cur, slot):
            for idx in range(max(len(nxt), len(cur))):
                if idx < len(nxt):
                    issue(qw_nxt, k_next, nxt[idx][0], 1 - slot, nxt[idx][1])
                if idx < len(cur):
                    consume(vt, cur[idx][0], slot, cur[idx][1])

        n_full = qi * sub

        def body(i, carry):
            for t in range(ATT_UNROLL):
                j = ATT_UNROLL * i + t
                step(qw, key_tile(j + 1), all_groups, vt_ref[0, 0, j], all_groups, t % 2)
            return carry

        if n_full:
            lax.fori_loop(0, n_full // ATT_UNROLL, body, 0)

        def diag_groups(jj):
            return [(g, n == jj) for g, (_, n) in enumerate(groups) if n >= jj]

        for jj in range(sub):
            cur = diag_groups(jj) if jj == 0 else [(g, False) for g, _ in diag_groups(jj)]
            vt = vt_ref[0, 0, n_full + jj]
            if jj + 1 < sub:
                step(qw, key_tile(n_full + jj + 1), diag_groups(jj + 1), vt, cur, jj % 2)
            elif last:
                step(None, None, [], vt, cur, jj % 2)
            else:
                step(qw_next, key_tile(0), all_groups, vt, cur, jj % 2)
        epilogue(qi)
        if not last:
            reset_stats()

    k_first = key_tile(0)
    qw_first = q_weights(0)
    for g, _ in all_groups:
        issue(qw_first, k_first, g, 0, False)
    reset_stats()
    for qi in range(n_q):
        query_tile(qi)


def _diff_attn(qt, k, vt, gate, lq1, lk1, lq2, lk2, subln_g):
    B, H, S, _ = k.shape
    tq, tk, tn = ATT_TQ, ATT_TK, ATT_TN
    assert tk == tn and ATT_UNROLL % 2 == 0 and (tq // tk) % ATT_UNROLL == 0
    n_col_groups = 2 * (tq // tn)
    lam_spec = pl.BlockSpec((1, DA_HEAD_DIM), lambda b, h: (0, 0))
    return pl.pallas_call(
        _diff_attn_kernel,
        grid=(B, H),
        in_specs=[
            pl.BlockSpec((1, 1, S // tn, LANE, tn), lambda b, h: (b, h, 0, 0, 0)),
            pl.BlockSpec((1, 1, S, LANE), lambda b, h: (b, h, 0, 0)),
            pl.BlockSpec((1, 1, S // tk, ATT_VROWS, tk), lambda b, h: (b, h, 0, 0, 0)),
            pl.BlockSpec((1, S, LANE), lambda b, h: (b, 0, h)),
            lam_spec, lam_spec, lam_spec, lam_spec,
            pl.BlockSpec((DA_V_DIM, 1), lambda b, h: (0, 0)),
        ],
        out_specs=pl.BlockSpec((1, S, LANE), lambda b, h: (b, 0, h)),
        out_shape=jax.ShapeDtypeStruct((B, S, DA_WIDTH), BF16),
        scratch_shapes=[
            pltpu.VMEM((n_col_groups, ATT_VROWS, tn), F32),
            pltpu.VMEM((n_col_groups, 1, tn), F32),
            pltpu.VMEM((2, n_col_groups, tk, tn), F32),
            pltpu.VMEM((2, n_col_groups, 1, tn), F32),
        ],
        compiler_params=pltpu.CompilerParams(
            dimension_semantics=("parallel", "parallel"),
            vmem_limit_bytes=VMEM_LIMIT),
        name="diff_attn",
    )(qt, k, vt, gate, lq1.reshape(1, -1), lk1.reshape(1, -1), lq2.reshape(1, -1),
      lk2.reshape(1, -1), subln_g.reshape(DA_V_DIM, 1))


def _gla_kernel(q_ref, k_ref, v_ref, la_ref, gate_ref, g_ref, o_ref, state_ref):
    C = GLA_CHUNK

    @pl.when(pl.program_id(1) == 0)
    def _():
        state_ref[...] = jnp.zeros_like(state_ref)

    r = lax.broadcasted_iota(jnp.int32, (C, C), 0)
    c = lax.broadcasted_iota(jnp.int32, (C, C), 1)
    causal = c <= r
    tri = jnp.where(causal, 1.0, 0.0).astype(BF16)
    tn = (((0,), (0,)), ((), ()))
    nt = (((1,), (1,)), ((), ()))
    chunks = range(GLA_ROWS // C)
    heads = range(GLA_HEADS)
    items = [(ci, h) for ci in chunks for h in heads]
    rows = {ci: slice(ci * C, (ci + 1) * C) for ci in chunks}
    ks = {h: slice(h * GLA_K_DIM, (h + 1) * GLA_K_DIM) for h in heads}
    vs = {h: slice(h * GLA_V_DIM, (h + 1) * GLA_V_DIM) for h in heads}

    b = {}
    for ci, h in items:
        la = la_ref[0, rows[ci], ks[h]]
        la_hi = la.astype(BF16)
        la_lo = (la - la_hi.astype(F32)).astype(BF16)
        b[ci, h] = (jnp.dot(tri, la_hi, preferred_element_type=F32)
                    + jnp.dot(tri, la_lo, preferred_element_type=F32))
    v = {(ci, h): v_ref[0, rows[ci], vs[h]] for ci, h in items}
    q_dec, k_inv, k_tail, decay = {}, {}, {}, {}
    for ci, h in items:
        bb = b[ci, h]
        b_last = bb[C - 1:C, :]
        q = q_ref[0, rows[ci], ks[h]].astype(F32)
        k = k_ref[0, rows[ci], ks[h]].astype(F32)
        q_dec[ci, h] = (q * jnp.exp2(bb)).astype(BF16)
        k_inv[ci, h] = (k * jnp.exp2(-bb)).astype(BF16)
        k_tail[ci, h] = (k * jnp.exp2(b_last - bb)).astype(BF16)
        decay[ci, h] = jnp.exp2(b_last)
    attn = {i: jnp.where(causal, lax.dot_general(q_dec[i], k_inv[i], nt,
                                                 preferred_element_type=F32), 0.0).astype(BF16)
            for i in items}
    kv_t = {i: lax.dot_general(v[i], k_tail[i], tn, preferred_element_type=F32)
            for i in items}
    intra = {i: jnp.dot(attn[i], v[i], preferred_element_type=F32) for i in items}

    for ci in chunks:
        o = {}
        for h in heads:
            state_t = state_ref[h]
            o[h] = intra[ci, h] + lax.dot_general(q_dec[ci, h], state_t.astype(BF16), nt,
                                                  preferred_element_type=F32)
            state_ref[h] = state_t * decay[ci, h] + kv_t[ci, h]
        for h in heads:
            ms = jnp.mean(o[h] * o[h], axis=-1, keepdims=True)
            y = o[h] * lax.rsqrt(ms + NORM_EPS) * g_ref[...]
            o_ref[0, rows[ci], vs[h]] = (y * gate_ref[0, rows[ci], vs[h]].astype(F32)).astype(BF16)


def _gla(qb, kb, vb, la, gate, gla_g):
    B, S, _ = qb.shape
    tg = GLA_ROWS
    row3 = lambda width: pl.BlockSpec((1, tg, width), lambda b, i: (b, i, 0))
    return pl.pallas_call(
        _gla_kernel,
        grid=(B, S // tg),
        in_specs=[row3(GLA_QK_WIDTH), row3(GLA_QK_WIDTH), row3(GLA_WIDTH),
                  row3(GLA_QK_WIDTH), row3(GLA_WIDTH),
                  pl.BlockSpec((1, GLA_V_DIM), lambda b, i: (0, 0))],
        out_specs=row3(GLA_WIDTH),
        out_shape=jax.ShapeDtypeStruct((B, S, GLA_WIDTH), BF16),
        scratch_shapes=[pltpu.VMEM((GLA_HEADS, GLA_V_DIM, GLA_K_DIM), F32)],
        compiler_params=pltpu.CompilerParams(
            dimension_semantics=("parallel", "arbitrary"),
            vmem_limit_bytes=VMEM_LIMIT),
        name="gla",
    )(qb, kb, vb, la, gate, gla_g.reshape(1, GLA_V_DIM))


def _out_proj_kernel(oa_ref, ob_ref, w_ref, x_ref, g_ref, o_ref):
    y = (jnp.dot(oa_ref[0], w_ref[:DA_WIDTH, :], preferred_element_type=F32)
         + jnp.dot(ob_ref[0], w_ref[DA_WIDTH:, :], preferred_element_type=F32))
    ms = jnp.mean(y * y, axis=-1, keepdims=True)
    o_ref[0] = x_ref[0] + y * lax.rsqrt(ms + NORM_EPS) * g_ref[...]


def _out_proj(oa, ob, w_out, x, post_g):
    B, S, D = x.shape
    tm = OUT_ROWS
    row3 = lambda width: pl.BlockSpec((1, tm, width), lambda b, i: (b, i, 0))
    const = dict(pipeline_mode=pl.Buffered(1))
    return pl.pallas_call(
        _out_proj_kernel,
        grid=(B, S // tm),
        in_specs=[row3(DA_WIDTH), row3(GLA_WIDTH),
                  pl.BlockSpec(w_out.shape, lambda b, i: (0, 0), **const),
                  row3(D),
                  pl.BlockSpec((1, D), lambda b, i: (0, 0), **const)],
        out_specs=row3(D),
        out_shape=jax.ShapeDtypeStruct((B, S, D), F32),
        compiler_params=pltpu.CompilerParams(
            dimension_semantics=("parallel", "parallel"),
            vmem_limit_bytes=VMEM_LIMIT),
        name="out_proj",
    )(oa, ob, w_out.astype(BF16), x, post_g.reshape(1, D))


def kernel(x, pre_norm_g, post_norm_g, w_in, w_gk_up, b_gk, lambda_q1, lambda_k1,
           lambda_q2, lambda_k2, attn_subln_g, gla_norm_g, w_out):
    assert x.shape[-1] == D_MODEL and pre_norm_g.shape[0] == 1
    qt, vt, k, ga, qb, kb, vb, gb, la = _in_proj(
        x, pre_norm_g[0], w_in[0], w_gk_up[0], b_gk[0])
    oa = _diff_attn(qt, k, vt, ga, lambda_q1[0], lambda_k1[0], lambda_q2[0],
                    lambda_k2[0], attn_subln_g[0])
    ob = _gla(qb, kb, vb, la, gb, gla_norm_g[0])
    return _out_proj(oa, ob, w_out[0], x, post_norm_g[0])
```

```python
import math

import jax
import jax.numpy as jnp
from jax import lax
from jax.experimental import pallas as pl
from jax.experimental.pallas import tpu as pltpu

D_MODEL = 1024
DA_HEADS = 8
DA_HEAD_DIM = 64
DA_V_DIM = 128
DA_WIDTH = DA_HEADS * DA_V_DIM
GLA_HEADS = 4
GLA_K_DIM = 128
GLA_V_DIM = 256
GLA_QK_WIDTH = GLA_HEADS * GLA_K_DIM
GLA_WIDTH = GLA_HEADS * GLA_V_DIM
GLA_GATE_RANK = 16
GLA_GATE_NORMALIZER = 16.0
GLA_CHUNK = 64
NORM_EPS = 1e-6
LAM_INIT = 0.8 - 0.6 * math.exp(-0.3 * 0)
IN_COLS = dict(qa=0, ka=1024, va=2048, ga=3072, qb=4096, kb=4608, vb=5120, gb=6144, gk=7168)

LANE = 128
BF16_SUBLANES = 16
VMEM_LIMIT = 56 * 1024 * 1024

PROJ_ROWS = 512
ATT_TQ = 4096
ATT_TK = 256
ATT_VROWS = DA_V_DIM + BF16_SUBLANES
ATT_TN = 256
ATT_UNROLL = 2
GLA_ROWS = 512
OUT_ROWS = 1024
NEG_BIG = -1e30

BF16 = jnp.bfloat16
F32 = jnp.float32


def _silu(x):
    return x * (1.0 / (1.0 + jnp.exp(-x)))


def _log_sigmoid(x):
    return jnp.minimum(x, 0.0) - jnp.log(1.0 + jnp.exp(-jnp.abs(x)))


def _in_proj_kernel(x_ref, g_ref, wt_ref, w_ref, wgki_ref, wgk_ref, bgk_ref,
                    qt_ref, vt_ref, k_ref, ga_ref, qb_ref, kb_ref, vb_ref,
                    gb_ref, la_ref):
    x = x_ref[0]
    ms = jnp.mean(x * x, axis=-1, keepdims=True)
    h = (x * lax.rsqrt(ms + NORM_EPS) * g_ref[...]).astype(BF16)

    nt = (((1,), (1,)), ((), ()))
    zt = lax.dot_general(wt_ref[...], h, nt, preferred_element_type=F32)
    q_scale = DA_HEAD_DIM ** -0.5 * math.log2(math.e)
    ones = jnp.ones((BF16_SUBLANES, ATT_TK), BF16)
    for hh in range(DA_HEADS):
        lo = hh * LANE
        for blk in range(x.shape[0] // ATT_TK):
            cs = slice(blk * ATT_TK, (blk + 1) * ATT_TK)
            qt_ref[0, hh, blk] = (zt[lo:lo + LANE, cs] * q_scale).astype(BF16)
            vt_ref[0, hh, blk, :DA_V_DIM, :] = zt[DA_WIDTH + lo:DA_WIDTH + lo + LANE, cs].astype(BF16)
            vt_ref[0, hh, blk, DA_V_DIM:, :] = ones

    def proj(name, width):
        c0 = IN_COLS[name]
        return jnp.dot(h, w_ref[:, c0:c0 + width], preferred_element_type=F32)

    zk = proj("ka", DA_WIDTH)
    for hh in range(DA_HEADS):
        k_ref[0, hh] = zk[:, hh * LANE:(hh + 1) * LANE].astype(BF16)
    ga_ref[0] = _silu(proj("ga", DA_WIDTH)).astype(BF16)
    zqk = proj("qb", 2 * GLA_QK_WIDTH)
    qb_ref[0] = (zqk[:, :GLA_QK_WIDTH] * (GLA_K_DIM ** -0.5)).astype(BF16)
    kb_ref[0] = zqk[:, GLA_QK_WIDTH:].astype(BF16)
    vb_ref[0] = proj("vb", GLA_WIDTH).astype(BF16)
    gb_ref[0] = _silu(proj("gb", GLA_WIDTH)).astype(BF16)
    gk_low = jnp.dot(h, wgki_ref[...], preferred_element_type=F32).astype(BF16)
    gk = jnp.dot(gk_low, wgk_ref[...], preferred_element_type=F32) + bgk_ref[...]
    la_ref[0] = _log_sigmoid(gk) * (math.log2(math.e) / GLA_GATE_NORMALIZER)


def _in_proj(x, pre_g, w_in, w_gk_up, b_gk):
    B, S, D = x.shape
    tm = PROJ_ROWS
    nblk = tm // ATT_TK

    def sl(w, name, width):
        return w[:, IN_COLS[name]:IN_COLS[name] + width]

    w = w_in.astype(BF16)
    wt = jnp.concatenate([sl(w, "qa", DA_WIDTH), sl(w, "va", DA_WIDTH)], axis=1).T
    wgki = jnp.pad(sl(w, "gk", GLA_GATE_RANK), ((0, 0), (0, LANE - GLA_GATE_RANK)))
    wgk = jnp.pad(w_gk_up, ((0, LANE - GLA_GATE_RANK), (0, 0))).astype(BF16)
    bgk = b_gk.reshape(1, GLA_QK_WIDTH)
    g = pre_g.reshape(1, D)

    const = dict(pipeline_mode=pl.Buffered(1))
    in_specs = [
        pl.BlockSpec((1, tm, D), lambda b, i: (b, i, 0)),
        pl.BlockSpec((1, D), lambda b, i: (0, 0), **const),
        pl.BlockSpec(wt.shape, lambda b, i: (0, 0), **const),
        pl.BlockSpec(w.shape, lambda b, i: (0, 0), **const),
        pl.BlockSpec(wgki.shape, lambda b, i: (0, 0), **const),
        pl.BlockSpec(wgk.shape, lambda b, i: (0, 0), **const),
        pl.BlockSpec(bgk.shape, lambda b, i: (0, 0), **const),
    ]
    row3 = lambda width: pl.BlockSpec((1, tm, width), lambda b, i: (b, i, 0))
    out_specs = [
        pl.BlockSpec((1, DA_HEADS, nblk, LANE, ATT_TK), lambda b, i: (b, 0, i, 0, 0)),
        pl.BlockSpec((1, DA_HEADS, nblk, ATT_VROWS, ATT_TK), lambda b, i: (b, 0, i, 0, 0)),
        pl.BlockSpec((1, DA_HEADS, tm, LANE), lambda b, i: (b, 0, i, 0)),
        row3(DA_WIDTH),
        row3(GLA_QK_WIDTH), row3(GLA_QK_WIDTH), row3(GLA_WIDTH), row3(GLA_WIDTH),
        row3(GLA_QK_WIDTH),
    ]
    out_shape = [
        jax.ShapeDtypeStruct((B, DA_HEADS, S // ATT_TK, LANE, ATT_TK), BF16),
        jax.ShapeDtypeStruct((B, DA_HEADS, S // ATT_TK, ATT_VROWS, ATT_TK), BF16),
        jax.ShapeDtypeStruct((B, DA_HEADS, S, LANE), BF16),
        jax.ShapeDtypeStruct((B, S, DA_WIDTH), BF16),
        jax.ShapeDtypeStruct((B, S, GLA_QK_WIDTH), BF16),
        jax.ShapeDtypeStruct((B, S, GLA_QK_WIDTH), BF16),
        jax.ShapeDtypeStruct((B, S, GLA_WIDTH), BF16),
        jax.ShapeDtypeStruct((B, S, GLA_WIDTH), BF16),
        jax.ShapeDtypeStruct((B, S, GLA_QK_WIDTH), F32),
    ]
    return pl.pallas_call(
        _in_proj_kernel,
        grid=(B, S // tm),
        in_specs=in_specs,
        out_specs=out_specs,
        out_shape=out_shape,
        compiler_params=pltpu.CompilerParams(
            dimension_semantics=("parallel", "parallel"),
            vmem_limit_bytes=VMEM_LIMIT),
        name="in_proj",
    )(x, g, wt, w, wgki, wgk, bgk)


def _diff_attn_kernel(qt_ref, k_ref, vt_ref, gate_ref, lq1_ref, lk1_ref, lq2_ref,
                      lk2_ref, g_ref, o_ref, acc_ref, m_ref, s_ref, c_ref):
    tq, tk, tn = ATT_TQ, ATT_TK, ATT_TN
    n_tiles = tq // tn
    sub = tq // tk
    n_q = o_ref.shape[1] // tq
    groups = [(mp, n) for mp in range(2) for n in range(n_tiles)]
    all_groups = [(g, False) for g in range(len(groups))]
    row = lax.broadcasted_iota(jnp.int32, (LANE, tn), 0)
    map_rows = (row < DA_HEAD_DIM, row >= DA_HEAD_DIM)
    causal = (lax.broadcasted_iota(jnp.int32, (tk, tn), 0)
              <= lax.broadcasted_iota(jnp.int32, (tk, tn), 1))
    lam = (jnp.exp(jnp.sum(lq1_ref[...] * lk1_ref[...], axis=-1, keepdims=True))
           - jnp.exp(jnp.sum(lq2_ref[...] * lk2_ref[...], axis=-1, keepdims=True))
           + LAM_INIT)
    gain = jnp.broadcast_to(g_ref[...] * (1.0 - LAM_INIT), (DA_V_DIM, tn))

    def aligned(start, size):
        return pl.ds(start if isinstance(start, int) else pl.multiple_of(start, size), size)

    def key_tile(j):
        return k_ref[0, 0, aligned(j * tk, tk), :]

    def epilogue(qi):
        for n in range(n_tiles):
            a1, a2 = acc_ref[n], acc_ref[n_tiles + n]
            l1, l2 = a1[DA_V_DIM:DA_V_DIM + 1], a2[DA_V_DIM:DA_V_DIM + 1]
            o = a1[:DA_V_DIM] * (1.0 / l1) - a2[:DA_V_DIM] * (lam / l2)
            msq = jnp.mean(o * o, axis=0, keepdims=True)
            y = o * lax.rsqrt(msq + NORM_EPS) * gain
            rows = aligned(qi * tq + n * tn, tn)
            o_ref[0, rows, :] = (y.T * gate_ref[0, rows, :].astype(F32)).astype(BF16)

    def q_weights(qi):
        out = []
        for mp, n in groups:
            blk = qt_ref[0, 0, qi * n_tiles + n]
            out.append(jnp.where(map_rows[mp], blk, jnp.zeros_like(blk)))
        return out

    def issue(qw, k, g, slot, masked):
        s = jnp.dot(k, qw[g], preferred_element_type=F32)
        if masked:
            s = jnp.where(causal, s, NEG_BIG)
        s_ref[slot, g] = s
        c_ref[slot, g] = jnp.max(s, axis=0, keepdims=True)

    def reset_stats():
        acc_ref[...] = jnp.zeros_like(acc_ref)
        m_ref[...] = jnp.full_like(m_ref, NEG_BIG)

    def query_tile(qi, carry):
        qw = q_weights(qi)
        qw_next = q_weights(jnp.minimum(qi + 1, n_q - 1))

        def consume(vt, g, slot, mask_now):
            s, c = s_ref[slot, g], c_ref[slot, g]
            if mask_now:
                s = jnp.where(causal, s, NEG_BIG)
                c = jnp.max(s, axis=0, keepdims=True)
            m_old = m_ref[g]
            m_new = jnp.maximum(m_old, c)
            alpha = jnp.exp2(m_old - m_new)
            p = jnp.exp2(s - m_new).astype(BF16)
            acc_ref[g] = alpha * acc_ref[g] + jnp.dot(vt, p, preferred_element_type=F32)
            m_ref[g] = m_new

        def step(qw_nxt, k_next, nxt, vt, cur, slot):
            for idx in range(max(len(nxt), len(cur))):
                if idx < len(nxt):
                    issue(qw_nxt, k_next, nxt[idx][0], 1 - slot, nxt[idx][1])
                if idx < len(cur):
                    consume(vt, cur[idx][0], slot, cur[idx][1])

        n_full = qi * sub

        def body(i, carry):
            for t in range(ATT_UNROLL):
                j = ATT_UNROLL * i + t
                step(qw, key_tile(j + 1), all_groups, vt_ref[0, 0, j], all_groups, t % 2)
            return carry

        lax.fori_loop(0, n_full // ATT_UNROLL, body, 0)

        def diag_groups(jj):
            return [(g, n == jj) for g, (_, n) in enumerate(groups) if n >= jj]

        for jj in range(sub):
            cur = diag_groups(jj) if jj == 0 else [(g, False) for g, _ in diag_groups(jj)]
            vt = vt_ref[0, 0, n_full + jj]
            if jj + 1 < sub:
                step(qw, key_tile(n_full + jj + 1), diag_groups(jj + 1), vt, cur, jj % 2)
            else:
                step(qw_next, key_tile(0), all_groups, vt, cur, jj % 2)
        epilogue(qi)
        reset_stats()
        return carry

    k_first = key_tile(0)
    qw_first = q_weights(0)
    for g, _ in all_groups:
        issue(qw_first, k_first, g, 0, False)
    reset_stats()
    lax.fori_loop(0, n_q, query_tile, 0)


def _diff_attn(qt, k, vt, gate, lq1, lk1, lq2, lk2, subln_g):
    B, H, S, _ = k.shape
    tq, tk, tn = ATT_TQ, ATT_TK, ATT_TN
    assert tk == tn and ATT_UNROLL % 2 == 0 and (tq // tk) % ATT_UNROLL == 0
    n_col_groups = 2 * (tq // tn)
    lam_spec = pl.BlockSpec((1, DA_HEAD_DIM), lambda b, h: (0, 0))
    return pl.pallas_call(
        _diff_attn_kernel,
        grid=(B, H),
        in_specs=[
            pl.BlockSpec((1, 1, S // tn, LANE, tn), lambda b, h: (b, h, 0, 0, 0)),
            pl.BlockSpec((1, 1, S, LANE), lambda b, h: (b, h, 0, 0)),
            pl.BlockSpec((1, 1, S // tk, ATT_VROWS, tk), lambda b, h: (b, h, 0, 0, 0)),
            pl.BlockSpec((1, S, LANE), lambda b, h: (b, 0, h)),
            lam_spec, lam_spec, lam_spec, lam_spec,
            pl.BlockSpec((DA_V_DIM, 1), lambda b, h: (0, 0)),
        ],
        out_specs=pl.BlockSpec((1, S, LANE), lambda b, h: (b, 0, h)),
        out_shape=jax.ShapeDtypeStruct((B, S, DA_WIDTH), BF16),
        scratch_shapes=[
            pltpu.VMEM((n_col_groups, ATT_VROWS, tn), F32),
            pltpu.VMEM((n_col_groups, 1, tn), F32),
            pltpu.VMEM((2, n_col_groups, tk, tn), F32),
            pltpu.VMEM((2, n_col_groups, 1, tn), F32),
        ],
        compiler_params=pltpu.CompilerParams(
            dimension_semantics=("parallel", "parallel"),
            vmem_limit_bytes=VMEM_LIMIT),
        name="diff_attn",
    )(qt, k, vt, gate, lq1.reshape(1, -1), lk1.reshape(1, -1), lq2.reshape(1, -1),
      lk2.reshape(1, -1), subln_g.reshape(DA_V_DIM, 1))


def _gla_kernel(q_ref, k_ref, v_ref, la_ref, gate_ref, g_ref, o_ref, state_ref):
    C = GLA_CHUNK

    @pl.when(pl.program_id(1) == 0)
    def _():
        state_ref[...] = jnp.zeros_like(state_ref)

    r = lax.broadcasted_iota(jnp.int32, (C, C), 0)
    c = lax.broadcasted_iota(jnp.int32, (C, C), 1)
    causal = c <= r
    tri = jnp.where(causal, 1.0, 0.0).astype(BF16)
    tn = (((0,), (0,)), ((), ()))
    nt = (((1,), (1,)), ((), ()))
    chunks = range(GLA_ROWS // C)
    heads = range(GLA_HEADS)
    items = [(ci, h) for ci in chunks for h in heads]
    rows = {ci: slice(ci * C, (ci + 1) * C) for ci in chunks}
    ks = {h: slice(h * GLA_K_DIM, (h + 1) * GLA_K_DIM) for h in heads}
    vs = {h: slice(h * GLA_V_DIM, (h + 1) * GLA_V_DIM) for h in heads}

    b = {}
    for ci, h in items:
        la = la_ref[0, rows[ci], ks[h]]
        la_hi = la.astype(BF16)
        la_lo = (la - la_hi.astype(F32)).astype(BF16)
        b[ci, h] = (jnp.dot(tri, la_hi, preferred_element_type=F32)
                    + jnp.dot(tri, la_lo, preferred_element_type=F32))
    v = {(ci, h): v_ref[0, rows[ci], vs[h]] for ci, h in items}
    q_dec, k_inv, k_tail, decay = {}, {}, {}, {}
    for ci, h in items:
        bb = b[ci, h]
        b_last = bb[C - 1:C, :]
        q = q_ref[0, rows[ci], ks[h]].astype(F32)
        k = k_ref[0, rows[ci], ks[h]].astype(F32)
        q_dec[ci, h] = (q * jnp.exp2(bb)).astype(BF16)
        k_inv[ci, h] = (k * jnp.exp2(-bb)).astype(BF16)
        k_tail[ci, h] = (k * jnp.exp2(b_last - bb)).astype(BF16)
        decay[ci, h] = jnp.exp2(b_last)
    attn = {i: jnp.where(causal, lax.dot_general(q_dec[i], k_inv[i], nt,
                                                 preferred_element_type=F32), 0.0).astype(BF16)
            for i in items}
    kv_t = {i: lax.dot_general(v[i], k_tail[i], tn, preferred_element_type=F32)
            for i in items}
    intra = {i: jnp.dot(attn[i], v[i], preferred_element_type=F32) for i in items}

    for ci in chunks:
        o = {}
        for h in heads:
            state_t = state_ref[h]
            o[h] = intra[ci, h] + lax.dot_general(q_dec[ci, h], state_t.astype(BF16), nt,
                                                  preferred_element_type=F32)
            state_ref[h] = state_t * decay[ci, h] + kv_t[ci, h]
        for h in heads:
            ms = jnp.mean(o[h] * o[h], axis=-1, keepdims=True)
            y = o[h] * lax.rsqrt(ms + NORM_EPS) * g_ref[...]
            o_ref[0, rows[ci], vs[h]] = (y * gate_ref[0, rows[ci], vs[h]].astype(F32)).astype(BF16)


def _gla(qb, kb, vb, la, gate, gla_g):
    B, S, _ = qb.shape
    tg = GLA_ROWS
    row3 = lambda width: pl.BlockSpec((1, tg, width), lambda b, i: (b, i, 0))
    return pl.pallas_call(
        _gla_kernel,
        grid=(B, S // tg),
        in_specs=[row3(GLA_QK_WIDTH), row3(GLA_QK_WIDTH), row3(GLA_WIDTH),
                  row3(GLA_QK_WIDTH), row3(GLA_WIDTH),
                  pl.BlockSpec((1, GLA_V_DIM), lambda b, i: (0, 0))],
        out_specs=row3(GLA_WIDTH),
        out_shape=jax.ShapeDtypeStruct((B, S, GLA_WIDTH), BF16),
        scratch_shapes=[pltpu.VMEM((GLA_HEADS, GLA_V_DIM, GLA_K_DIM), F32)],
        compiler_params=pltpu.CompilerParams(
            dimension_semantics=("parallel", "arbitrary"),
            vmem_limit_bytes=VMEM_LIMIT),
        name="gla",
    )(qb, kb, vb, la, gate, gla_g.reshape(1, GLA_V_DIM))


def _out_proj_kernel(oa_ref, ob_ref, w_ref, x_ref, g_ref, o_ref):
    y = (jnp.dot(oa_ref[0], w_ref[:DA_WIDTH, :], preferred_element_type=F32)
         + jnp.dot(ob_ref[0], w_ref[DA_WIDTH:, :], preferred_element_type=F32))
    ms = jnp.mean(y * y, axis=-1, keepdims=True)
    o_ref[0] = x_ref[0] + y * lax.rsqrt(ms + NORM_EPS) * g_ref[...]


def _out_proj(oa, ob, w_out, x, post_g):
    B, S, D = x.shape
    tm = OUT_ROWS
    row3 = lambda width: pl.BlockSpec((1, tm, width), lambda b, i: (b, i, 0))
    const = dict(pipeline_mode=pl.Buffered(1))
    return pl.pallas_call(
        _out_proj_kernel,
        grid=(B, S // tm),
        in_specs=[row3(DA_WIDTH), row3(GLA_WIDTH),
                  pl.BlockSpec(w_out.shape, lambda b, i: (0, 0), **const),
                  row3(D),
                  pl.BlockSpec((1, D), lambda b, i: (0, 0), **const)],
        out_specs=row3(D),
        out_shape=jax.ShapeDtypeStruct((B, S, D), F32),
        compiler_params=pltpu.CompilerParams(
            dimension_semantics=("parallel", "parallel"),
            vmem_limit_bytes=VMEM_LIMIT),
        name="out_proj",
    )(oa, ob, w_out.astype(BF16), x, post_g.reshape(1, D))


def kernel(x, pre_norm_g, post_norm_g, w_in, w_gk_up, b_gk, lambda_q1, lambda_k1,
           lambda_q2, lambda_k2, attn_subln_g, gla_norm_g, w_out):
    assert x.shape[-1] == D_MODEL and pre_norm_g.shape[0] == 1
    qt, vt, k, ga, qb, kb, vb, gb, la = _in_proj(
        x, pre_norm_g[0], w_in[0], w_gk_up[0], b_gk[0])
    oa = _diff_attn(qt, k, vt, ga, lambda_q1[0], lambda_k1[0], lambda_q2[0],
                    lambda_k2[0], attn_subln_g[0])
    ob = _gla(qb, kb, vb, la, gb, gla_norm_g[0])
    return _out_proj(oa, ob, w_out[0], x, post_norm_g[0])
```

```python
import math

import jax
import jax.numpy as jnp
from jax import lax
from jax.experimental import pallas as pl
from jax.experimental.pallas import tpu as pltpu

D_MODEL = 1024
DA_HEADS = 8
DA_HEAD_DIM = 64
DA_V_DIM = 128
DA_WIDTH = DA_HEADS * DA_V_DIM
GLA_HEADS = 4
GLA_K_DIM = 128
GLA_V_DIM = 256
GLA_QK_WIDTH = GLA_HEADS * GLA_K_DIM
GLA_WIDTH = GLA_HEADS * GLA_V_DIM
GLA_GATE_RANK = 16
GLA_GATE_NORMALIZER = 16.0
GLA_CHUNK = 64
NORM_EPS = 1e-6
LAM_INIT = 0.8 - 0.6 * math.exp(-0.3 * 0)
IN_COLS = dict(qa=0, ka=1024, va=2048, ga=3072, qb=4096, kb=4608, vb=5120, gb=6144, gk=7168)

LANE = 128
BF16_SUBLANES = 16
VMEM_LIMIT = 56 * 1024 * 1024

PROJ_ROWS = 512
ATT_TQ = 4096
ATT_TK = 256
ATT_VROWS = DA_V_DIM + BF16_SUBLANES
ATT_TN = 256
ATT_UNROLL = 4
GLA_ROWS = 512
OUT_ROWS = 1024
NEG_BIG = -1e30

BF16 = jnp.bfloat16
F32 = jnp.float32


def _silu(x):
    return x * (1.0 / (1.0 + jnp.exp(-x)))


def _log_sigmoid(x):
    return jnp.minimum(x, 0.0) - jnp.log(1.0 + jnp.exp(-jnp.abs(x)))


def _in_proj_kernel(x_ref, g_ref, wt_ref, w_ref, wgki_ref, wgk_ref, bgk_ref,
                    qt_ref, vt_ref, k_ref, ga_ref, qb_ref, kb_ref, vb_ref,
                    gb_ref, la_ref):
    x = x_ref[0]
    ms = jnp.mean(x * x, axis=-1, keepdims=True)
    h = (x * lax.rsqrt(ms + NORM_EPS) * g_ref[...]).astype(BF16)

    nt = (((1,), (1,)), ((), ()))
    zt = lax.dot_general(wt_ref[...], h, nt, preferred_element_type=F32)
    q_scale = DA_HEAD_DIM ** -0.5 * math.log2(math.e)
    ones = jnp.ones((BF16_SUBLANES, ATT_TK), BF16)
    for hh in range(DA_HEADS):
        lo = hh * LANE
        for blk in range(x.shape[0] // ATT_TK):
            cs = slice(blk * ATT_TK, (blk + 1) * ATT_TK)
            qt_ref[0, hh, blk] = (zt[lo:lo + LANE, cs] * q_scale).astype(BF16)
            vt_ref[0, hh, blk, :DA_V_DIM, :] = zt[DA_WIDTH + lo:DA_WIDTH + lo + LANE, cs].astype(BF16)
            vt_ref[0, hh, blk, DA_V_DIM:, :] = ones

    def proj(name, width):
        c0 = IN_COLS[name]
        return jnp.dot(h, w_ref[:, c0:c0 + width], preferred_element_type=F32)

    zk = proj("ka", DA_WIDTH)
    for hh in range(DA_HEADS):
        k_ref[0, hh] = zk[:, hh * LANE:(hh + 1) * LANE].astype(BF16)
    ga_ref[0] = _silu(proj("ga", DA_WIDTH)).astype(BF16)
    zqk = proj("qb", 2 * GLA_QK_WIDTH)
    qb_ref[0] = (zqk[:, :GLA_QK_WIDTH] * (GLA_K_DIM ** -0.5)).astype(BF16)
    kb_ref[0] = zqk[:, GLA_QK_WIDTH:].astype(BF16)
    vb_ref[0] = proj("vb", GLA_WIDTH).astype(BF16)
    gb_ref[0] = _silu(proj("gb", GLA_WIDTH)).astype(BF16)
    gk_low = jnp.dot(h, wgki_ref[...], preferred_element_type=F32).astype(BF16)
    gk = jnp.dot(gk_low, wgk_ref[...], preferred_element_type=F32) + bgk_ref[...]
    la_ref[0] = _log_sigmoid(gk) * (math.log2(math.e) / GLA_GATE_NORMALIZER)


def _in_proj(x, pre_g, w_in, w_gk_up, b_gk):
    B, S, D = x.shape
    tm = PROJ_ROWS
    nblk = tm // ATT_TK

    def sl(w, name, width):
        return w[:, IN_COLS[name]:IN_COLS[name] + width]

    w = w_in.astype(BF16)
    wt = jnp.concatenate([sl(w, "qa", DA_WIDTH), sl(w, "va", DA_WIDTH)], axis=1).T
    wgki = jnp.pad(sl(w, "gk", GLA_GATE_RANK), ((0, 0), (0, LANE - GLA_GATE_RANK)))
    wgk = jnp.pad(w_gk_up, ((0, LANE - GLA_GATE_RANK), (0, 0))).astype(BF16)
    bgk = b_gk.reshape(1, GLA_QK_WIDTH)
    g = pre_g.reshape(1, D)

    const = dict(pipeline_mode=pl.Buffered(1))
    in_specs = [
        pl.BlockSpec((1, tm, D), lambda b, i: (b, i, 0)),
        pl.BlockSpec((1, D), lambda b, i: (0, 0), **const),
        pl.BlockSpec(wt.shape, lambda b, i: (0, 0), **const),
        pl.BlockSpec(w.shape, lambda b, i: (0, 0), **const),
        pl.BlockSpec(wgki.shape, lambda b, i: (0, 0), **const),
        pl.BlockSpec(wgk.shape, lambda b, i: (0, 0), **const),
        pl.BlockSpec(bgk.shape, lambda b, i: (0, 0), **const),
    ]
    row3 = lambda width: pl.BlockSpec((1, tm, width), lambda b, i: (b, i, 0))
    out_specs = [
        pl.BlockSpec((1, DA_HEADS, nblk, LANE, ATT_TK), lambda b, i: (b, 0, i, 0, 0)),
        pl.BlockSpec((1, DA_HEADS, nblk, ATT_VROWS, ATT_TK), lambda b, i: (b, 0, i, 0, 0)),
        pl.BlockSpec((1, DA_HEADS, tm, LANE), lambda b, i: (b, 0, i, 0)),
        row3(DA_WIDTH),
        row3(GLA_QK_WIDTH), row3(GLA_QK_WIDTH), row3(GLA_WIDTH), row3(GLA_WIDTH),
        row3(GLA_QK_WIDTH),
    ]
    out_shape = [
        jax.ShapeDtypeStruct((B, DA_HEADS, S // ATT_TK, LANE, ATT_TK), BF16),
        jax.ShapeDtypeStruct((B, DA_HEADS, S // ATT_TK, ATT_VROWS, ATT_TK), BF16),
        jax.ShapeDtypeStruct((B, DA_HEADS, S, LANE), BF16),
        jax.ShapeDtypeStruct((B, S, DA_WIDTH), BF16),
        jax.ShapeDtypeStruct((B, S, GLA_QK_WIDTH), BF16),
        jax.ShapeDtypeStruct((B, S, GLA_QK_WIDTH), BF16),
        jax.ShapeDtypeStruct((B, S, GLA_WIDTH), BF16),
        jax.ShapeDtypeStruct((B, S, GLA_WIDTH), BF16),
        jax.ShapeDtypeStruct((B, S, GLA_QK_WIDTH), F32),
    ]
    return pl.pallas_call(
        _in_proj_kernel,
        grid=(B, S // tm),
        in_specs=in_specs,
        out_specs=out_specs,
        out_shape=out_shape,
        compiler_params=pltpu.CompilerParams(
            dimension_semantics=("parallel", "parallel"),
            vmem_limit_bytes=VMEM_LIMIT),
        name="in_proj",
    )(x, g, wt, w, wgki, wgk, bgk)


def _diff_attn_kernel(qt_ref, k_ref, vt_ref, gate_ref, lq1_ref, lk1_ref, lq2_ref,
                      lk2_ref, g_ref, o_ref, acc_ref, m_ref, s_ref, c_ref):
    tq, tk, tn = ATT_TQ, ATT_TK, ATT_TN
    n_tiles = tq // tn
    sub = tq // tk
    n_q = o_ref.shape[1] // tq
    groups = [(mp, n) for mp in range(2) for n in range(n_tiles)]
    all_groups = [(g, False) for g in range(len(groups))]
    row = lax.broadcasted_iota(jnp.int32, (LANE, tn), 0)
    map_rows = (row < DA_HEAD_DIM, row >= DA_HEAD_DIM)
    causal = (lax.broadcasted_iota(jnp.int32, (tk, tn), 0)
              <= lax.broadcasted_iota(jnp.int32, (tk, tn), 1))
    lam = (jnp.exp(jnp.sum(lq1_ref[...] * lk1_ref[...], axis=-1, keepdims=True))
           - jnp.exp(jnp.sum(lq2_ref[...] * lk2_ref[...], axis=-1, keepdims=True))
           + LAM_INIT)
    gain = jnp.broadcast_to(g_ref[...] * (1.0 - LAM_INIT), (DA_V_DIM, tn))

    def aligned(start, size):
        return pl.ds(start if isinstance(start, int) else pl.multiple_of(start, size), size)

    def key_tile(j):
        return k_ref[0, 0, aligned(j * tk, tk), :]

    def epilogue(qi):
        for n in range(n_tiles):
            a1, a2 = acc_ref[n], acc_ref[n_tiles + n]
            l1, l2 = a1[DA_V_DIM:DA_V_DIM + 1], a2[DA_V_DIM:DA_V_DIM + 1]
            o = a1[:DA_V_DIM] * (1.0 / l1) - a2[:DA_V_DIM] * (lam / l2)
            msq = jnp.mean(o * o, axis=0, keepdims=True)
            y = o * lax.rsqrt(msq + NORM_EPS) * gain
            rows = aligned(qi * tq + n * tn, tn)
            o_ref[0, rows, :] = (y.T * gate_ref[0, rows, :].astype(F32)).astype(BF16)

    def q_weights(qi):
        out = []
        for mp, n in groups:
            blk = qt_ref[0, 0, qi * n_tiles + n]
            out.append(jnp.where(map_rows[mp], blk, jnp.zeros_like(blk)))
        return out

    def issue(qw, k, g, slot, masked):
        s = jnp.dot(k, qw[g], preferred_element_type=F32)
        if masked:
            s = jnp.where(causal, s, NEG_BIG)
        s_ref[slot, g] = s
        c_ref[slot, g] = jnp.max(s, axis=0, keepdims=True)

    def reset_stats():
        acc_ref[...] = jnp.zeros_like(acc_ref)
        m_ref[...] = jnp.full_like(m_ref, NEG_BIG)

    def query_tile(qi, carry):
        qw = q_weights(qi)
        qw_next = q_weights(jnp.minimum(qi + 1, n_q - 1))

        def consume(vt, g, slot, mask_now):
            s, c = s_ref[slot, g], c_ref[slot, g]
            if mask_now:
                s = jnp.where(causal, s, NEG_BIG)
                c = jnp.max(s, axis=0, keepdims=True)
            m_old = m_ref[g]
            m_new = jnp.maximum(m_old, c)
            alpha = jnp.exp2(m_old - m_new)
            p = jnp.exp2(s - m_new).astype(BF16)
            acc_ref[g] = alpha * acc_ref[g] + jnp.dot(vt, p, preferred_element_type=F32)
            m_ref[g] = m_new

        def step(qw_nxt, k_next, nxt, vt, cur, slot):
            for idx in range(max(len(nxt), len(cur))):
                if idx < len(nxt):
                    issue(qw_nxt, k_next, nxt[idx][0], 1 - slot, nxt[idx][1])
                if idx < len(cur):
                    consume(vt, cur[idx][0], slot, cur[idx][1])

        n_full = qi * sub

        def body(i, carry):
            for t in range(ATT_UNROLL):
                j = ATT_UNROLL * i + t
                step(qw, key_tile(j + 1), all_groups, vt_ref[0, 0, j], all_groups, t % 2)
            return carry

        lax.fori_loop(0, n_full // ATT_UNROLL, body, 0)

        def diag_groups(jj):
            return [(g, n == jj) for g, (_, n) in enumerate(groups) if n >= jj]

        for jj in range(sub):
            cur = diag_groups(jj) if jj == 0 else [(g, False) for g, _ in diag_groups(jj)]
            vt = vt_ref[0, 0, n_full + jj]
            if jj + 1 < sub:
                step(qw, key_tile(n_full + jj + 1), diag_groups(jj + 1), vt, cur, jj % 2)
            else:
                step(qw_next, key_tile(0), all_groups, vt, cur, jj % 2)
        epilogue(qi)
        reset_stats()
        return carry

    k_first = key_tile(0)
    qw_first = q_weights(0)
    for g, _ in all_groups:
        issue(qw_first, k_first, g, 0, False)
    reset_stats()
    lax.fori_loop(0, n_q, query_tile, 0)


def _diff_attn(qt, k, vt, gate, lq1, lk1, lq2, lk2, subln_g):
    B, H, S, _ = k.shape
    tq, tk, tn = ATT_TQ, ATT_TK, ATT_TN
    assert tk == tn and ATT_UNROLL % 2 == 0 and (tq // tk) % ATT_UNROLL == 0
    n_col_groups = 2 * (tq // tn)
    lam_spec = pl.BlockSpec((1, DA_HEAD_DIM), lambda b, h: (0, 0))
    return pl.pallas_call(
        _diff_attn_kernel,
        grid=(B, H),
        in_specs=[
            pl.BlockSpec((1, 1, S // tn, LANE, tn), lambda b, h: (b, h, 0, 0, 0)),
            pl.BlockSpec((1, 1, S, LANE), lambda b, h: (b, h, 0, 0)),
            pl.BlockSpec((1, 1, S // tk, ATT_VROWS, tk), lambda b, h: (b, h, 0, 0, 0)),
            pl.BlockSpec((1, S, LANE), lambda b, h: (b, 0, h)),
            lam_spec, lam_spec, lam_spec, lam_spec,
            pl.BlockSpec((DA_V_DIM, 1), lambda b, h: (0, 0)),
        ],
        out_specs=pl.BlockSpec((1, S, LANE), lambda b, h: (b, 0, h)),
        out_shape=jax.ShapeDtypeStruct((B, S, DA_WIDTH), BF16),
        scratch_shapes=[
            pltpu.VMEM((n_col_groups, ATT_VROWS, tn), F32),
            pltpu.VMEM((n_col_groups, 1, tn), F32),
            pltpu.VMEM((2, n_col_groups, tk, tn), F32),
            pltpu.VMEM((2, n_col_groups, 1, tn), F32),
        ],
        compiler_params=pltpu.CompilerParams(
            dimension_semantics=("parallel", "parallel"),
            vmem_limit_bytes=VMEM_LIMIT),
        name="diff_attn",
    )(qt, k, vt, gate, lq1.reshape(1, -1), lk1.reshape(1, -1), lq2.reshape(1, -1),
      lk2.reshape(1, -1), subln_g.reshape(DA_V_DIM, 1))


def _gla_kernel(q_ref, k_ref, v_ref, la_ref, gate_ref, g_ref, o_ref, state_ref):
    C = GLA_CHUNK

    @pl.when(pl.program_id(1) == 0)
    def _():
        state_ref[...] = jnp.zeros_like(state_ref)

    r = lax.broadcasted_iota(jnp.int32, (C, C), 0)
    c = lax.broadcasted_iota(jnp.int32, (C, C), 1)
    causal = c <= r
    tri = jnp.where(causal, 1.0, 0.0).astype(BF16)
    tn = (((0,), (0,)), ((), ()))
    nt = (((1,), (1,)), ((), ()))
    chunks = range(GLA_ROWS // C)
    heads = range(GLA_HEADS)
    items = [(ci, h) for ci in chunks for h in heads]
    rows = {ci: slice(ci * C, (ci + 1) * C) for ci in chunks}
    ks = {h: slice(h * GLA_K_DIM, (h + 1) * GLA_K_DIM) for h in heads}
    vs = {h: slice(h * GLA_V_DIM, (h + 1) * GLA_V_DIM) for h in heads}

    b = {}
    for ci, h in items:
        la = la_ref[0, rows[ci], ks[h]]
        la_hi = la.astype(BF16)
        la_lo = (la - la_hi.astype(F32)).astype(BF16)
        b[ci, h] = (jnp.dot(tri, la_hi, preferred_element_type=F32)
                    + jnp.dot(tri, la_lo, preferred_element_type=F32))
    v = {(ci, h): v_ref[0, rows[ci], vs[h]] for ci, h in items}
    q_dec, k_inv, k_tail, decay = {}, {}, {}, {}
    for ci, h in items:
        bb = b[ci, h]
        b_last = bb[C - 1:C, :]
        q = q_ref[0, rows[ci], ks[h]].astype(F32)
        k = k_ref[0, rows[ci], ks[h]].astype(F32)
        q_dec[ci, h] = (q * jnp.exp2(bb)).astype(BF16)
        k_inv[ci, h] = (k * jnp.exp2(-bb)).astype(BF16)
        k_tail[ci, h] = (k * jnp.exp2(b_last - bb)).astype(BF16)
        decay[ci, h] = jnp.exp2(b_last)
    attn = {i: jnp.where(causal, lax.dot_general(q_dec[i], k_inv[i], nt,
                                                 preferred_element_type=F32), 0.0).astype(BF16)
            for i in items}
    kv_t = {i: lax.dot_general(v[i], k_tail[i], tn, preferred_element_type=F32)
            for i in items}
    intra = {i: jnp.dot(attn[i], v[i], preferred_element_type=F32) for i in items}

    for ci in chunks:
        o = {}
        for h in heads:
            state_t = state_ref[h]
            o[h] = intra[ci, h] + lax.dot_general(q_dec[ci, h], state_t.astype(BF16), nt,
                                                  preferred_element_type=F32)
            state_ref[h] = state_t * decay[ci, h] + kv_t[ci, h]
        for h in heads:
            ms = jnp.mean(o[h] * o[h], axis=-1, keepdims=True)
            y = o[h] * lax.rsqrt(ms + NORM_EPS) * g_ref[...]
            o_ref[0, rows[ci], vs[h]] = (y * gate_ref[0, rows[ci], vs[h]].astype(F32)).astype(BF16)


def _gla(qb, kb, vb, la, gate, gla_g):
    B, S, _ = qb.shape
    tg = GLA_ROWS
    row3 = lambda width: pl.BlockSpec((1, tg, width), lambda b, i: (b, i, 0))
    return pl.pallas_call(
        _gla_kernel,
        grid=(B, S // tg),
        in_specs=[row3(GLA_QK_WIDTH), row3(GLA_QK_WIDTH), row3(GLA_WIDTH),
                  row3(GLA_QK_WIDTH), row3(GLA_WIDTH),
                  pl.BlockSpec((1, GLA_V_DIM), lambda b, i: (0, 0))],
        out_specs=row3(GLA_WIDTH),
        out_shape=jax.ShapeDtypeStruct((B, S, GLA_WIDTH), BF16),
        scratch_shapes=[pltpu.VMEM((GLA_HEADS, GLA_V_DIM, GLA_K_DIM), F32)],
        compiler_params=pltpu.CompilerParams(
            dimension_semantics=("parallel", "arbitrary"),
            vmem_limit_bytes=VMEM_LIMIT),
        name="gla",
    )(qb, kb, vb, la, gate, gla_g.reshape(1, GLA_V_DIM))


def _out_proj_kernel(oa_ref, ob_ref, w_ref, x_ref, g_ref, o_ref):
    y = (jnp.dot(oa_ref[0], w_ref[:DA_WIDTH, :], preferred_element_type=F32)
         + jnp.dot(ob_ref[0], w_ref[DA_WIDTH:, :], preferred_element_type=F32))
    ms = jnp.mean(y * y, axis=-1, keepdims=True)
    o_ref[0] = x_ref[0] + y * lax.rsqrt(ms + NORM_EPS) * g_ref[...]


def _out_proj(oa, ob, w_out, x, post_g):
    B, S, D = x.shape
    tm = OUT_ROWS
    row3 = lambda width: pl.BlockSpec((1, tm, width), lambda b, i: (b, i, 0))
    const = dict(pipeline_mode=pl.Buffered(1))
    return pl.pallas_call(
        _out_proj_kernel,
        grid=(B, S // tm),
        in_specs=[row3(DA_WIDTH), row3(GLA_WIDTH),
                  pl.BlockSpec(w_out.shape, lambda b, i: (0, 0), **const),
                  row3(D),
                  pl.BlockSpec((1, D), lambda b, i: (0, 0), **const)],
        out_specs=row3(D),
        out_shape=jax.ShapeDtypeStruct((B, S, D), F32),
        compiler_params=pltpu.CompilerParams(
            dimension_semantics=("parallel", "parallel"),
            vmem_limit_bytes=VMEM_LIMIT),
        name="out_proj",
    )(oa, ob, w_out.astype(BF16), x, post_g.reshape(1, D))


def kernel(x, pre_norm_g, post_norm_g, w_in, w_gk_up, b_gk, lambda_q1, lambda_k1,
           lambda_q2, lambda_k2, attn_subln_g, gla_norm_g, w_out):
    assert x.shape[-1] == D_MODEL and pre_norm_g.shape[0] == 1
    qt, vt, k, ga, qb, kb, vb, gb, la = _in_proj(
        x, pre_norm_g[0], w_in[0], w_gk_up[0], b_gk[0])
    oa = _diff_attn(qt, k, vt, ga, lambda_q1[0], lambda_k1[0], lambda_q2[0],
                    lambda_k2[0], attn_subln_g[0])
    ob = _gla(qb, kb, vb, la, gb, gla_norm_g[0])
    return _out_proj(oa, ob, w_out[0], x, post_norm_g[0])
```

```python
import math

import jax
import jax.numpy as jnp
from jax import lax
from jax.experimental import pallas as pl
from jax.experimental.pallas import tpu as pltpu

D_MODEL = 1024
DA_HEADS = 8
DA_HEAD_DIM = 64
DA_V_DIM = 128
DA_WIDTH = DA_HEADS * DA_V_DIM
GLA_HEADS = 4
GLA_K_DIM = 128
GLA_V_DIM = 256
GLA_QK_WIDTH = GLA_HEADS * GLA_K_DIM
GLA_WIDTH = GLA_HEADS * GLA_V_DIM
GLA_GATE_RANK = 16
GLA_GATE_NORMALIZER = 16.0
GLA_CHUNK = 64
NORM_EPS = 1e-6
LAM_INIT = 0.8 - 0.6 * math.exp(-0.3 * 0)
IN_COLS = dict(qa=0, ka=1024, va=2048, ga=3072, qb=4096, kb=4608, vb=5120, gb=6144, gk=7168)

LANE = 128
BF16_SUBLANES = 16
VMEM_LIMIT = 56 * 1024 * 1024

PROJ_ROWS = 512
ATT_TQ = 4096
ATT_TK = 256
ATT_VROWS = DA_V_DIM + BF16_SUBLANES
ATT_TN = 256
ATT_UNROLL = 4
GLA_ROWS = 512
OUT_ROWS = 1024
NEG_BIG = -1e30

BF16 = jnp.bfloat16
F32 = jnp.float32


def _silu(x):
    return x * (1.0 / (1.0 + jnp.exp(-x)))


def _log_sigmoid(x):
    return jnp.minimum(x, 0.0) - jnp.log(1.0 + jnp.exp(-jnp.abs(x)))


def _in_proj_kernel(x_ref, g_ref, wt_ref, w_ref, wgki_ref, wgk_ref, bgk_ref,
                    qt_ref, vt_ref, k_ref, ga_ref, qb_ref, kb_ref, vb_ref,
                    gb_ref, la_ref):
    x = x_ref[0]
    ms = jnp.mean(x * x, axis=-1, keepdims=True)
    h = (x * lax.rsqrt(ms + NORM_EPS) * g_ref[...]).astype(BF16)

    nt = (((1,), (1,)), ((), ()))
    zt = lax.dot_general(wt_ref[...], h, nt, preferred_element_type=F32)
    q_scale = DA_HEAD_DIM ** -0.5 * math.log2(math.e)
    ones = jnp.ones((BF16_SUBLANES, ATT_TK), BF16)
    for hh in range(DA_HEADS):
        lo = hh * LANE
        for blk in range(x.shape[0] // ATT_TK):
            cs = slice(blk * ATT_TK, (blk + 1) * ATT_TK)
            qt_ref[0, hh, blk] = (zt[lo:lo + LANE, cs] * q_scale).astype(BF16)
            vt_ref[0, hh, blk, :DA_V_DIM, :] = zt[DA_WIDTH + lo:DA_WIDTH + lo + LANE, cs].astype(BF16)
            vt_ref[0, hh, blk, DA_V_DIM:, :] = ones

    def proj(name, width):
        c0 = IN_COLS[name]
        return jnp.dot(h, w_ref[:, c0:c0 + width], preferred_element_type=F32)

    zk = proj("ka", DA_WIDTH)
    for hh in range(DA_HEADS):
        k_ref[0, hh] = zk[:, hh * LANE:(hh + 1) * LANE].astype(BF16)
    ga_ref[0] = _silu(proj("ga", DA_WIDTH)).astype(BF16)
    zqk = proj("qb", 2 * GLA_QK_WIDTH)
    qb_ref[0] = (zqk[:, :GLA_QK_WIDTH] * (GLA_K_DIM ** -0.5)).astype(BF16)
    kb_ref[0] = zqk[:, GLA_QK_WIDTH:].astype(BF16)
    vb_ref[0] = proj("vb", GLA_WIDTH).astype(BF16)
    gb_ref[0] = _silu(proj("gb", GLA_WIDTH)).astype(BF16)
    gk_low = jnp.dot(h, wgki_ref[...], preferred_element_type=F32).astype(BF16)
    gk = jnp.dot(gk_low, wgk_ref[...], preferred_element_type=F32) + bgk_ref[...]
    la_ref[0] = _log_sigmoid(gk) * (math.log2(math.e) / GLA_GATE_NORMALIZER)


def _in_proj(x, pre_g, w_in, w_gk_up, b_gk):
    B, S, D = x.shape
    tm = PROJ_ROWS
    nblk = tm // ATT_TK

    def sl(w, name, width):
        return w[:, IN_COLS[name]:IN_COLS[name] + width]

    w = w_in.astype(BF16)
    wt = jnp.concatenate([sl(w, "qa", DA_WIDTH), sl(w, "va", DA_WIDTH)], axis=1).T
    wgki = jnp.pad(sl(w, "gk", GLA_GATE_RANK), ((0, 0), (0, LANE - GLA_GATE_RANK)))
    wgk = jnp.pad(w_gk_up, ((0, LANE - GLA_GATE_RANK), (0, 0))).astype(BF16)
    bgk = b_gk.reshape(1, GLA_QK_WIDTH)
    g = pre_g.reshape(1, D)

    const = dict(pipeline_mode=pl.Buffered(1))
    in_specs = [
        pl.BlockSpec((1, tm, D), lambda b, i: (b, i, 0)),
        pl.BlockSpec((1, D), lambda b, i: (0, 0), **const),
        pl.BlockSpec(wt.shape, lambda b, i: (0, 0), **const),
        pl.BlockSpec(w.shape, lambda b, i: (0, 0), **const),
        pl.BlockSpec(wgki.shape, lambda b, i: (0, 0), **const),
        pl.BlockSpec(wgk.shape, lambda b, i: (0, 0), **const),
        pl.BlockSpec(bgk.shape, lambda b, i: (0, 0), **const),
    ]
    row3 = lambda width: pl.BlockSpec((1, tm, width), lambda b, i: (b, i, 0))
    out_specs = [
        pl.BlockSpec((1, DA_HEADS, nblk, LANE, ATT_TK), lambda b, i: (b, 0, i, 0, 0)),
        pl.BlockSpec((1, DA_HEADS, nblk, ATT_VROWS, ATT_TK), lambda b, i: (b, 0, i, 0, 0)),
        pl.BlockSpec((1, DA_HEADS, tm, LANE), lambda b, i: (b, 0, i, 0)),
        row3(DA_WIDTH),
        row3(GLA_QK_WIDTH), row3(GLA_QK_WIDTH), row3(GLA_WIDTH), row3(GLA_WIDTH),
        row3(GLA_QK_WIDTH),
    ]
    out_shape = [
        jax.ShapeDtypeStruct((B, DA_HEADS, S // ATT_TK, LANE, ATT_TK), BF16),
        jax.ShapeDtypeStruct((B, DA_HEADS, S // ATT_TK, ATT_VROWS, ATT_TK), BF16),
        jax.ShapeDtypeStruct((B, DA_HEADS, S, LANE), BF16),
        jax.ShapeDtypeStruct((B, S, DA_WIDTH), BF16),
        jax.ShapeDtypeStruct((B, S, GLA_QK_WIDTH), BF16),
        jax.ShapeDtypeStruct((B, S, GLA_QK_WIDTH), BF16),
        jax.ShapeDtypeStruct((B, S, GLA_WIDTH), BF16),
        jax.ShapeDtypeStruct((B, S, GLA_WIDTH), BF16),
        jax.ShapeDtypeStruct((B, S, GLA_QK_WIDTH), F32),
    ]
    return pl.pallas_call(
        _in_proj_kernel,
        grid=(B, S // tm),
        in_specs=in_specs,
        out_specs=out_specs,
        out_shape=out_shape,
        compiler_params=pltpu.CompilerParams(
            dimension_semantics=("parallel", "parallel"),
            vmem_limit_bytes=VMEM_LIMIT),
        name="in_proj",
    )(x, g, wt, w, wgki, wgk, bgk)


def _diff_attn_kernel(qt_ref, k_ref, vt_ref, gate_ref, lq1_ref, lk1_ref, lq2_ref,
                      lk2_ref, g_ref, o_ref, acc_ref, m_ref, s_ref, c_ref):
    tq, tk, tn = ATT_TQ, ATT_TK, ATT_TN
    n_tiles = tq // tn
    sub = tq // tk
    n_q = o_ref.shape[1] // tq
    groups = [(mp, n) for mp in range(2) for n in range(n_tiles)]
    all_groups = [(g, False) for g in range(len(groups))]
    row = lax.broadcasted_iota(jnp.int32, (LANE, tn), 0)
    map_rows = (row < DA_HEAD_DIM, row >= DA_HEAD_DIM)
    causal = (lax.broadcasted_iota(jnp.int32, (tk, tn), 0)
              <= lax.broadcasted_iota(jnp.int32, (tk, tn), 1))
    lam = (jnp.exp(jnp.sum(lq1_ref[...] * lk1_ref[...], axis=-1, keepdims=True))
           - jnp.exp(jnp.sum(lq2_ref[...] * lk2_ref[...], axis=-1, keepdims=True))
           + LAM_INIT)
    gain = jnp.broadcast_to(g_ref[...] * (1.0 - LAM_INIT), (DA_V_DIM, tn))

    def aligned(start, size):
        return pl.ds(start if isinstance(start, int) else pl.multiple_of(start, size), size)

    def key_tile(j):
        return k_ref[0, 0, aligned(j * tk, tk), :]

    def epilogue(qi, n):
        a1, a2 = acc_ref[n], acc_ref[n_tiles + n]
        l1, l2 = a1[DA_V_DIM:DA_V_DIM + 1], a2[DA_V_DIM:DA_V_DIM + 1]
        o = a1[:DA_V_DIM] * (1.0 / l1) - a2[:DA_V_DIM] * (lam / l2)
        msq = jnp.mean(o * o, axis=0, keepdims=True)
        y = o * lax.rsqrt(msq + NORM_EPS) * gain
        rows = aligned(qi * tq + n * tn, tn)
        o_ref[0, rows, :] = (y.T * gate_ref[0, rows, :].astype(F32)).astype(BF16)

    def q_weights(qi):
        out = []
        for mp, n in groups:
            blk = qt_ref[0, 0, qi * n_tiles + n]
            out.append(jnp.where(map_rows[mp], blk, jnp.zeros_like(blk)))
        return out

    def issue(qw, k, g, slot, masked):
        s = jnp.dot(k, qw[g], preferred_element_type=F32)
        if masked:
            s = jnp.where(causal, s, NEG_BIG)
        s_ref[slot, g] = s
        c_ref[slot, g] = jnp.max(s, axis=0, keepdims=True)

    def reset_stats():
        acc_ref[...] = jnp.zeros_like(acc_ref)
        m_ref[...] = jnp.full_like(m_ref, NEG_BIG)

    def query_tile(qi, carry):
        qw = q_weights(qi)
        qw_next = q_weights(jnp.minimum(qi + 1, n_q - 1))

        def consume(vt, g, slot, mask_now):
            s, c = s_ref[slot, g], c_ref[slot, g]
            if mask_now:
                s = jnp.where(causal, s, NEG_BIG)
                c = jnp.max(s, axis=0, keepdims=True)
            m_old = m_ref[g]
            m_new = jnp.maximum(m_old, c)
            alpha = jnp.exp2(m_old - m_new)
            p = jnp.exp2(s - m_new).astype(BF16)
            acc_ref[g] = alpha * acc_ref[g] + jnp.dot(vt, p, preferred_element_type=F32)
            m_ref[g] = m_new

        def step(issues, vt, cur, slot):
            for idx in range(max(len(issues), len(cur))):
                if idx < len(issues):
                    issue(*issues[idx])
                if idx < len(cur):
                    consume(vt, cur[idx][0], slot, cur[idx][1])

        n_full = qi * sub

        def body(i, carry):
            for t in range(ATT_UNROLL):
                j = ATT_UNROLL * i + t
                k_next = key_tile(j + 1)
                step([(qw, k_next, g, 1 - t % 2, False) for g, _ in all_groups],
                     vt_ref[0, 0, j], all_groups, t % 2)
            return carry

        lax.fori_loop(0, n_full // ATT_UNROLL, body, 0)

        def diag_groups(jj):
            return [(g, n == jj) for g, (_, n) in enumerate(groups) if n >= jj]

        k_zero = key_tile(0)
        for jj in range(sub):
            slot = jj % 2
            cur = diag_groups(jj) if jj == 0 else [(g, False) for g, _ in diag_groups(jj)]
            issues = []
            if jj + 1 < sub:
                k_next = key_tile(n_full + jj + 1)
                issues += [(qw, k_next, g, 1 - slot, masked) for g, masked in diag_groups(jj + 1)]
            finished = [g for g, (_, n) in enumerate(groups)
                        if n == jj - 1 or (jj == sub - 1 and n == jj)]
            issues += [(qw_next, k_zero, g, 0, False) for g in finished]
            step(issues, vt_ref[0, 0, n_full + jj], cur, slot)
            epilogue(qi, jj)
        reset_stats()
        return carry

    k_first = key_tile(0)
    qw_first = q_weights(0)
    for g, _ in all_groups:
        issue(qw_first, k_first, g, 0, False)
    reset_stats()
    lax.fori_loop(0, n_q, query_tile, 0)


def _diff_attn(qt, k, vt, gate, lq1, lk1, lq2, lk2, subln_g):
    B, H, S, _ = k.shape
    tq, tk, tn = ATT_TQ, ATT_TK, ATT_TN
    assert tk == tn and ATT_UNROLL % 2 == 0 and (tq // tk) % ATT_UNROLL == 0
    n_col_groups = 2 * (tq // tn)
    lam_spec = pl.BlockSpec((1, DA_HEAD_DIM), lambda b, h: (0, 0))
    return pl.pallas_call(
        _diff_attn_kernel,
        grid=(B, H),
        in_specs=[
            pl.BlockSpec((1, 1, S // tn, LANE, tn), lambda b, h: (b, h, 0, 0, 0)),
            pl.BlockSpec((1, 1, S, LANE), lambda b, h: (b, h, 0, 0)),
            pl.BlockSpec((1, 1, S // tk, ATT_VROWS, tk), lambda b, h: (b, h, 0, 0, 0)),
            pl.BlockSpec((1, S, LANE), lambda b, h: (b, 0, h)),
            lam_spec, lam_spec, lam_spec, lam_spec,
            pl.BlockSpec((DA_V_DIM, 1), lambda b, h: (0, 0)),
        ],
        out_specs=pl.BlockSpec((1, S, LANE), lambda b, h: (b, 0, h)),
        out_shape=jax.ShapeDtypeStruct((B, S, DA_WIDTH), BF16),
        scratch_shapes=[
            pltpu.VMEM((n_col_groups, ATT_VROWS, tn), F32),
            pltpu.VMEM((n_col_groups, 1, tn), F32),
            pltpu.VMEM((2, n_col_groups, tk, tn), F32),
            pltpu.VMEM((2, n_col_groups, 1, tn), F32),
        ],
        compiler_params=pltpu.CompilerParams(
            dimension_semantics=("parallel", "parallel"),
            vmem_limit_bytes=VMEM_LIMIT),
        name="diff_attn",
    )(qt, k, vt, gate, lq1.reshape(1, -1), lk1.reshape(1, -1), lq2.reshape(1, -1),
      lk2.reshape(1, -1), subln_g.reshape(DA_V_DIM, 1))


def _gla_kernel(q_ref, k_ref, v_ref, la_ref, gate_ref, g_ref, o_ref, state_ref):
    C = GLA_CHUNK

    @pl.when(pl.program_id(1) == 0)
    def _():
        state_ref[...] = jnp.zeros_like(state_ref)

    r = lax.broadcasted_iota(jnp.int32, (C, C), 0)
    c = lax.broadcasted_iota(jnp.int32, (C, C), 1)
    causal = c <= r
    tri = jnp.where(causal, 1.0, 0.0).astype(BF16)
    tn = (((0,), (0,)), ((), ()))
    nt = (((1,), (1,)), ((), ()))
    chunks = range(GLA_ROWS // C)
    heads = range(GLA_HEADS)
    items = [(ci, h) for ci in chunks for h in heads]
    rows = {ci: slice(ci * C, (ci + 1) * C) for ci in chunks}
    ks = {h: slice(h * GLA_K_DIM, (h + 1) * GLA_K_DIM) for h in heads}
    vs = {h: slice(h * GLA_V_DIM, (h + 1) * GLA_V_DIM) for h in heads}

    b = {}
    for ci, h in items:
        la = la_ref[0, rows[ci], ks[h]]
        la_hi = la.astype(BF16)
        la_lo = (la - la_hi.astype(F32)).astype(BF16)
        b[ci, h] = (jnp.dot(tri, la_hi, preferred_element_type=F32)
                    + jnp.dot(tri, la_lo, preferred_element_type=F32))
    v = {(ci, h): v_ref[0, rows[ci], vs[h]] for ci, h in items}
    q_dec, k_inv, k_tail, decay = {}, {}, {}, {}
    for ci, h in items:
        bb = b[ci, h]
        b_last = bb[C - 1:C, :]
        q = q_ref[0, rows[ci], ks[h]].astype(F32)
        k = k_ref[0, rows[ci], ks[h]].astype(F32)
        q_dec[ci, h] = (q * jnp.exp2(bb)).astype(BF16)
        k_inv[ci, h] = (k * jnp.exp2(-bb)).astype(BF16)
        k_tail[ci, h] = (k * jnp.exp2(b_last - bb)).astype(BF16)
        decay[ci, h] = jnp.exp2(b_last)
    attn = {i: jnp.where(causal, lax.dot_general(q_dec[i], k_inv[i], nt,
                                                 preferred_element_type=F32), 0.0).astype(BF16)
            for i in items}
    kv_t = {i: lax.dot_general(v[i], k_tail[i], tn, preferred_element_type=F32)
            for i in items}
    intra = {i: jnp.dot(attn[i], v[i], preferred_element_type=F32) for i in items}

    for ci in chunks:
        o = {}
        for h in heads:
            state_t = state_ref[h]
            o[h] = intra[ci, h] + lax.dot_general(q_dec[ci, h], state_t.astype(BF16), nt,
                                                  preferred_element_type=F32)
            state_ref[h] = state_t * decay[ci, h] + kv_t[ci, h]
        for h in heads:
            ms = jnp.mean(o[h] * o[h], axis=-1, keepdims=True)
            y = o[h] * lax.rsqrt(ms + NORM_EPS) * g_ref[...]
            o_ref[0, rows[ci], vs[h]] = (y * gate_ref[0, rows[ci], vs[h]].astype(F32)).astype(BF16)


def _gla(qb, kb, vb, la, gate, gla_g):
    B, S, _ = qb.shape
    tg = GLA_ROWS
    row3 = lambda width: pl.BlockSpec((1, tg, width), lambda b, i: (b, i, 0))
    return pl.pallas_call(
        _gla_kernel,
        grid=(B, S // tg),
        in_specs=[row3(GLA_QK_WIDTH), row3(GLA_QK_WIDTH), row3(GLA_WIDTH),
                  row3(GLA_QK_WIDTH), row3(GLA_WIDTH),
                  pl.BlockSpec((1, GLA_V_DIM), lambda b, i: (0, 0))],
        out_specs=row3(GLA_WIDTH),
        out_shape=jax.ShapeDtypeStruct((B, S, GLA_WIDTH), BF16),
        scratch_shapes=[pltpu.VMEM((GLA_HEADS, GLA_V_DIM, GLA_K_DIM), F32)],
        compiler_params=pltpu.CompilerParams(
            dimension_semantics=("parallel", "arbitrary"),
            vmem_limit_bytes=VMEM_LIMIT),
        name="gla",
    )(qb, kb, vb, la, gate, gla_g.reshape(1, GLA_V_DIM))


def _out_proj_kernel(oa_ref, ob_ref, w_ref, x_ref, g_ref, o_ref):
    y = (jnp.dot(oa_ref[0], w_ref[:DA_WIDTH, :], preferred_element_type=F32)
         + jnp.dot(ob_ref[0], w_ref[DA_WIDTH:, :], preferred_element_type=F32))
    ms = jnp.mean(y * y, axis=-1, keepdims=True)
    o_ref[0] = x_ref[0] + y * lax.rsqrt(ms + NORM_EPS) * g_ref[...]


def _out_proj(oa, ob, w_out, x, post_g):
    B, S, D = x.shape
    tm = OUT_ROWS
    row3 = lambda width: pl.BlockSpec((1, tm, width), lambda b, i: (b, i, 0))
    const = dict(pipeline_mode=pl.Buffered(1))
    return pl.pallas_call(
        _out_proj_kernel,
        grid=(B, S // tm),
        in_specs=[row3(DA_WIDTH), row3(GLA_WIDTH),
                  pl.BlockSpec(w_out.shape, lambda b, i: (0, 0), **const),
                  row3(D),
                  pl.BlockSpec((1, D), lambda b, i: (0, 0), **const)],
        out_specs=row3(D),
        out_shape=jax.ShapeDtypeStruct((B, S, D), F32),
        compiler_params=pltpu.CompilerParams(
            dimension_semantics=("parallel", "parallel"),
            vmem_limit_bytes=VMEM_LIMIT),
        name="out_proj",
    )(oa, ob, w_out.astype(BF16), x, post_g.reshape(1, D))


def kernel(x, pre_norm_g, post_norm_g, w_in, w_gk_up, b_gk, lambda_q1, lambda_k1,
           lambda_q2, lambda_k2, attn_subln_g, gla_norm_g, w_out):
    assert x.shape[-1] == D_MODEL and pre_norm_g.shape[0] == 1
    qt, vt, k, ga, qb, kb, vb, gb, la = _in_proj(
        x, pre_norm_g[0], w_in[0], w_gk_up[0], b_gk[0])
    oa = _diff_attn(qt, k, vt, ga, lambda_q1[0], lambda_k1[0], lambda_q2[0],
                    lambda_k2[0], attn_subln_g[0])
    ob = _gla(qb, kb, vb, la, gb, gla_norm_g[0])
    return _out_proj(oa, ob, w_out[0], x, post_norm_g[0])
```

```python
import math

import jax
import jax.numpy as jnp
from jax import lax
from jax.experimental import pallas as pl
from jax.experimental.pallas import tpu as pltpu

D_MODEL = 1024
DA_HEADS = 8
DA_HEAD_DIM = 64
DA_V_DIM = 128
DA_WIDTH = DA_HEADS * DA_V_DIM
GLA_HEADS = 4
GLA_K_DIM = 128
GLA_V_DIM = 256
GLA_QK_WIDTH = GLA_HEADS * GLA_K_DIM
GLA_WIDTH = GLA_HEADS * GLA_V_DIM
GLA_GATE_RANK = 16
GLA_GATE_NORMALIZER = 16.0
GLA_CHUNK = 64
NORM_EPS = 1e-6
LAM_INIT = 0.8 - 0.6 * math.exp(-0.3 * 0)
IN_COLS = dict(qa=0, ka=1024, va=2048, ga=3072, qb=4096, kb=4608, vb=5120, gb=6144, gk=7168)

LANE = 128
BF16_SUBLANES = 16
VMEM_LIMIT = 56 * 1024 * 1024

PROJ_ROWS = 512
ATT_TQ = 4096
ATT_TK = 256
ATT_VROWS = DA_V_DIM + BF16_SUBLANES
ATT_TN = 256
ATT_UNROLL = 4
GLA_ROWS = 512
OUT_ROWS = 1024
NEG_BIG = -1e30

BF16 = jnp.bfloat16
F32 = jnp.float32


def _silu(x):
    return x * (1.0 / (1.0 + jnp.exp(-x)))


def _log_sigmoid(x):
    return jnp.minimum(x, 0.0) - jnp.log(1.0 + jnp.exp(-jnp.abs(x)))


def _in_proj_kernel(x_ref, g_ref, w_ref, wgki_ref, wgk_ref, bgk_ref,
                    qt_ref, vt_ref, k_ref, ga_ref, qb_ref, kb_ref, vb_ref,
                    gb_ref, la_ref):
    x = x_ref[0]
    ms = jnp.mean(x * x, axis=-1, keepdims=True)
    h = (x * lax.rsqrt(ms + NORM_EPS) * g_ref[...]).astype(BF16)

    def proj(name, width):
        c0 = IN_COLS[name]
        return jnp.dot(h, w_ref[:, c0:c0 + width], preferred_element_type=F32)

    zq, zv = proj("qa", DA_WIDTH), proj("va", DA_WIDTH)
    q_scale = DA_HEAD_DIM ** -0.5 * math.log2(math.e)
    ones = jnp.ones((BF16_SUBLANES, ATT_TK), BF16)
    for hh in range(DA_HEADS):
        cs = slice(hh * LANE, (hh + 1) * LANE)
        for blk in range(x.shape[0] // ATT_TK):
            rs = slice(blk * ATT_TK, (blk + 1) * ATT_TK)
            qt_ref[0, hh, blk] = (zq[rs, cs].T * q_scale).astype(BF16)
            vt_ref[0, hh, blk, :DA_V_DIM, :] = zv[rs, cs].T.astype(BF16)
            vt_ref[0, hh, blk, DA_V_DIM:, :] = ones

    zk = proj("ka", DA_WIDTH)
    for hh in range(DA_HEADS):
        k_ref[0, hh] = zk[:, hh * LANE:(hh + 1) * LANE].astype(BF16)
    ga_ref[0] = _silu(proj("ga", DA_WIDTH)).astype(BF16)
    zqk = proj("qb", 2 * GLA_QK_WIDTH)
    qb_ref[0] = (zqk[:, :GLA_QK_WIDTH] * (GLA_K_DIM ** -0.5)).astype(BF16)
    kb_ref[0] = zqk[:, GLA_QK_WIDTH:].astype(BF16)
    vb_ref[0] = proj("vb", GLA_WIDTH).astype(BF16)
    gb_ref[0] = _silu(proj("gb", GLA_WIDTH)).astype(BF16)
    gk_low = jnp.dot(h, wgki_ref[...], preferred_element_type=F32).astype(BF16)
    gk = jnp.dot(gk_low, wgk_ref[...], preferred_element_type=F32) + bgk_ref[...]
    la_ref[0] = _log_sigmoid(gk) * (math.log2(math.e) / GLA_GATE_NORMALIZER)


def _in_proj(x, pre_g, w_in, w_gk_up, b_gk):
    B, S, D = x.shape
    tm = PROJ_ROWS
    nblk = tm // ATT_TK

    def sl(w, name, width):
        return w[:, IN_COLS[name]:IN_COLS[name] + width]

    w = w_in.astype(BF16)
    wgki = jnp.pad(sl(w, "gk", GLA_GATE_RANK), ((0, 0), (0, LANE - GLA_GATE_RANK)))
    wgk = jnp.pad(w_gk_up, ((0, LANE - GLA_GATE_RANK), (0, 0))).astype(BF16)
    bgk = b_gk.reshape(1, GLA_QK_WIDTH)
    g = pre_g.reshape(1, D)

    const = dict(pipeline_mode=pl.Buffered(1))
    in_specs = [
        pl.BlockSpec((1, tm, D), lambda b, i: (b, i, 0)),
        pl.BlockSpec((1, D), lambda b, i: (0, 0), **const),
        pl.BlockSpec(w.shape, lambda b, i: (0, 0), **const),
        pl.BlockSpec(wgki.shape, lambda b, i: (0, 0), **const),
        pl.BlockSpec(wgk.shape, lambda b, i: (0, 0), **const),
        pl.BlockSpec(bgk.shape, lambda b, i: (0, 0), **const),
    ]
    row3 = lambda width: pl.BlockSpec((1, tm, width), lambda b, i: (b, i, 0))
    out_specs = [
        pl.BlockSpec((1, DA_HEADS, nblk, LANE, ATT_TK), lambda b, i: (b, 0, i, 0, 0)),
        pl.BlockSpec((1, DA_HEADS, nblk, ATT_VROWS, ATT_TK), lambda b, i: (b, 0, i, 0, 0)),
        pl.BlockSpec((1, DA_HEADS, tm, LANE), lambda b, i: (b, 0, i, 0)),
        row3(DA_WIDTH),
        row3(GLA_QK_WIDTH), row3(GLA_QK_WIDTH), row3(GLA_WIDTH), row3(GLA_WIDTH),
        row3(GLA_QK_WIDTH),
    ]
    out_shape = [
        jax.ShapeDtypeStruct((B, DA_HEADS, S // ATT_TK, LANE, ATT_TK), BF16),
        jax.ShapeDtypeStruct((B, DA_HEADS, S // ATT_TK, ATT_VROWS, ATT_TK), BF16),
        jax.ShapeDtypeStruct((B, DA_HEADS, S, LANE), BF16),
        jax.ShapeDtypeStruct((B, S, DA_WIDTH), BF16),
        jax.ShapeDtypeStruct((B, S, GLA_QK_WIDTH), BF16),
        jax.ShapeDtypeStruct((B, S, GLA_QK_WIDTH), BF16),
        jax.ShapeDtypeStruct((B, S, GLA_WIDTH), BF16),
        jax.ShapeDtypeStruct((B, S, GLA_WIDTH), BF16),
        jax.ShapeDtypeStruct((B, S, GLA_QK_WIDTH), F32),
    ]
    return pl.pallas_call(
        _in_proj_kernel,
        grid=(B, S // tm),
        in_specs=in_specs,
        out_specs=out_specs,
        out_shape=out_shape,
        compiler_params=pltpu.CompilerParams(
            dimension_semantics=("parallel", "parallel"),
            vmem_limit_bytes=VMEM_LIMIT),
        name="in_proj",
    )(x, g, w, wgki, wgk, bgk)


def _diff_attn_kernel(qt_ref, k_ref, vt_ref, gate_ref, lq1_ref, lk1_ref, lq2_ref,
                      lk2_ref, g_ref, o_ref, acc_ref, m_ref, s_ref, c_ref):
    tq, tk, tn = ATT_TQ, ATT_TK, ATT_TN
    n_tiles = tq // tn
    sub = tq // tk
    n_q = o_ref.shape[1] // tq
    groups = [(mp, n) for mp in range(2) for n in range(n_tiles)]
    all_groups = [(g, False) for g in range(len(groups))]
    row = lax.broadcasted_iota(jnp.int32, (LANE, tn), 0)
    map_rows = (row < DA_HEAD_DIM, row >= DA_HEAD_DIM)
    causal = (lax.broadcasted_iota(jnp.int32, (tk, tn), 0)
              <= lax.broadcasted_iota(jnp.int32, (tk, tn), 1))
    lam = (jnp.exp(jnp.sum(lq1_ref[...] * lk1_ref[...], axis=-1, keepdims=True))
           - jnp.exp(jnp.sum(lq2_ref[...] * lk2_ref[...], axis=-1, keepdims=True))
           + LAM_INIT)
    gain = jnp.broadcast_to(g_ref[...] * (1.0 - LAM_INIT), (DA_V_DIM, tn))

    def aligned(start, size):
        return pl.ds(start if isinstance(start, int) else pl.multiple_of(start, size), size)

    def key_tile(j):
        return k_ref[0, 0, aligned(j * tk, tk), :]

    def epilogue(qi, n):
        a1, a2 = acc_ref[n], acc_ref[n_tiles + n]
        l1, l2 = a1[DA_V_DIM:DA_V_DIM + 1], a2[DA_V_DIM:DA_V_DIM + 1]
        o = a1[:DA_V_DIM] * (1.0 / l1) - a2[:DA_V_DIM] * (lam / l2)
        msq = jnp.mean(o * o, axis=0, keepdims=True)
        y = o * lax.rsqrt(msq + NORM_EPS) * gain
        rows = aligned(qi * tq + n * tn, tn)
        o_ref[0, rows, :] = (y.T * gate_ref[0, rows, :].astype(F32)).astype(BF16)

    def q_weights(qi):
        out = []
        for mp, n in groups:
            blk = qt_ref[0, 0, qi * n_tiles + n]
            out.append(jnp.where(map_rows[mp], blk, jnp.zeros_like(blk)))
        return out

    def issue(qw, k, g, slot, masked):
        s = jnp.dot(k, qw[g], preferred_element_type=F32)
        if masked:
            s = jnp.where(causal, s, NEG_BIG)
        s_ref[slot, g] = s
        c_ref[slot, g] = jnp.max(s, axis=0, keepdims=True)

    def reset_stats():
        acc_ref[...] = jnp.zeros_like(acc_ref)
        m_ref[...] = jnp.full_like(m_ref, NEG_BIG)

    def query_tile(qi, carry):
        qw = q_weights(qi)
        qw_next = q_weights(jnp.minimum(qi + 1, n_q - 1))

        def consume(vt, g, slot, mask_now):
            s, c = s_ref[slot, g], c_ref[slot, g]
            if mask_now:
                s = jnp.where(causal, s, NEG_BIG)
                c = jnp.max(s, axis=0, keepdims=True)
            m_old = m_ref[g]
            m_new = jnp.maximum(m_old, c)
            alpha = jnp.exp2(m_old - m_new)
            p = jnp.exp2(s - m_new).astype(BF16)
            acc_ref[g] = alpha * acc_ref[g] + jnp.dot(vt, p, preferred_element_type=F32)
            m_ref[g] = m_new

        def step(issues, vt, cur, slot):
            for idx in range(max(len(issues), len(cur))):
                if idx < len(issues):
                    issue(*issues[idx])
                if idx < len(cur):
                    consume(vt, cur[idx][0], slot, cur[idx][1])

        n_full = qi * sub

        def body(i, carry):
            for t in range(ATT_UNROLL):
                j = ATT_UNROLL * i + t
                k_next = key_tile(j + 1)
                step([(qw, k_next, g, 1 - t % 2, False) for g, _ in all_groups],
                     vt_ref[0, 0, j], all_groups, t % 2)
            return carry

        lax.fori_loop(0, n_full // ATT_UNROLL, body, 0)

        def diag_groups(jj):
            return [(g, n == jj) for g, (_, n) in enumerate(groups) if n >= jj]

        k_zero = key_tile(0)
        for jj in range(sub):
            slot = jj % 2
            cur = diag_groups(jj) if jj == 0 else [(g, False) for g, _ in diag_groups(jj)]
            issues = []
            if jj + 1 < sub:
                k_next = key_tile(n_full + jj + 1)
                issues += [(qw, k_next, g, 1 - slot, masked) for g, masked in diag_groups(jj + 1)]
            finished = [g for g, (_, n) in enumerate(groups)
                        if n == jj - 1 or (jj == sub - 1 and n == jj)]
            issues += [(qw_next, k_zero, g, 0, False) for g in finished]
            step(issues, vt_ref[0, 0, n_full + jj], cur, slot)
            epilogue(qi, jj)
        reset_stats()
        return carry

    k_first = key_tile(0)
    qw_first = q_weights(0)
    for g, _ in all_groups:
        issue(qw_first, k_first, g, 0, False)
    reset_stats()
    lax.fori_loop(0, n_q, query_tile, 0)


def _diff_attn(qt, k, vt, gate, lq1, lk1, lq2, lk2, subln_g):
    B, H, S, _ = k.shape
    tq, tk, tn = ATT_TQ, ATT_TK, ATT_TN
    assert tk == tn and ATT_UNROLL % 2 == 0 and (tq // tk) % ATT_UNROLL == 0
    n_col_groups = 2 * (tq // tn)
    lam_spec = pl.BlockSpec((1, DA_HEAD_DIM), lambda b, h: (0, 0))
    return pl.pallas_call(
        _diff_attn_kernel,
        grid=(B, H),
        in_specs=[
            pl.BlockSpec((1, 1, S // tn, LANE, tn), lambda b, h: (b, h, 0, 0, 0)),
            pl.BlockSpec((1, 1, S, LANE), lambda b, h: (b, h, 0, 0)),
            pl.BlockSpec((1, 1, S // tk, ATT_VROWS, tk), lambda b, h: (b, h, 0, 0, 0)),
            pl.BlockSpec((1, S, LANE), lambda b, h: (b, 0, h)),
            lam_spec, lam_spec, lam_spec, lam_spec,
            pl.BlockSpec((DA_V_DIM, 1), lambda b, h: (0, 0)),
        ],
        out_specs=pl.BlockSpec((1, S, LANE), lambda b, h: (b, 0, h)),
        out_shape=jax.ShapeDtypeStruct((B, S, DA_WIDTH), BF16),
        scratch_shapes=[
            pltpu.VMEM((n_col_groups, ATT_VROWS, tn), F32),
            pltpu.VMEM((n_col_groups, 1, tn), F32),
            pltpu.VMEM((2, n_col_groups, tk, tn), F32),
            pltpu.VMEM((2, n_col_groups, 1, tn), F32),
        ],
        compiler_params=pltpu.CompilerParams(
            dimension_semantics=("parallel", "parallel"),
            vmem_limit_bytes=VMEM_LIMIT),
        name="diff_attn",
    )(qt, k, vt, gate, lq1.reshape(1, -1), lk1.reshape(1, -1), lq2.reshape(1, -1),
      lk2.reshape(1, -1), subln_g.reshape(DA_V_DIM, 1))


def _gla_kernel(q_ref, k_ref, v_ref, la_ref, gate_ref, g_ref, o_ref, state_ref):
    C = GLA_CHUNK

    @pl.when(pl.program_id(1) == 0)
    def _():
        state_ref[...] = jnp.zeros_like(state_ref)

    r = lax.broadcasted_iota(jnp.int32, (C, C), 0)
    c = lax.broadcasted_iota(jnp.int32, (C, C), 1)
    causal = c <= r
    tri = jnp.where(causal, 1.0, 0.0).astype(BF16)
    tn = (((0,), (0,)), ((), ()))
    nt = (((1,), (1,)), ((), ()))
    chunks = range(GLA_ROWS // C)
    heads = range(GLA_HEADS)
    items = [(ci, h) for ci in chunks for h in heads]
    rows = {ci: slice(ci * C, (ci + 1) * C) for ci in chunks}
    ks = {h: slice(h * GLA_K_DIM, (h + 1) * GLA_K_DIM) for h in heads}
    vs = {h: slice(h * GLA_V_DIM, (h + 1) * GLA_V_DIM) for h in heads}

    b = {}
    for ci, h in items:
        la = la_ref[0, rows[ci], ks[h]]
        la_hi = la.astype(BF16)
        la_lo = (la - la_hi.astype(F32)).astype(BF16)
        b[ci, h] = (jnp.dot(tri, la_hi, preferred_element_type=F32)
                    + jnp.dot(tri, la_lo, preferred_element_type=F32))
    v = {(ci, h): v_ref[0, rows[ci], vs[h]] for ci, h in items}
    q_dec, k_inv, k_tail, decay = {}, {}, {}, {}
    for ci, h in items:
        bb = b[ci, h]
        b_last = bb[C - 1:C, :]
        q = q_ref[0, rows[ci], ks[h]].astype(F32)
        k = k_ref[0, rows[ci], ks[h]].astype(F32)
        q_dec[ci, h] = (q * jnp.exp2(bb)).astype(BF16)
        k_inv[ci, h] = (k * jnp.exp2(-bb)).astype(BF16)
        k_tail[ci, h] = (k * jnp.exp2(b_last - bb)).astype(BF16)
        decay[ci, h] = jnp.exp2(b_last)
    attn = {i: jnp.where(causal, lax.dot_general(q_dec[i], k_inv[i], nt,
                                                 preferred_element_type=F32), 0.0).astype(BF16)
            for i in items}
    kv_t = {i: lax.dot_general(v[i], k_tail[i], tn, preferred_element_type=F32)
            for i in items}
    intra = {i: jnp.dot(attn[i], v[i], preferred_element_type=F32) for i in items}

    for ci in chunks:
        o = {}
        for h in heads:
            state_t = state_ref[h]
            o[h] = intra[ci, h] + lax.dot_general(q_dec[ci, h], state_t.astype(BF16), nt,
                                                  preferred_element_type=F32)
            state_ref[h] = state_t * decay[ci, h] + kv_t[ci, h]
        for h in heads:
            ms = jnp.mean(o[h] * o[h], axis=-1, keepdims=True)
            y = o[h] * lax.rsqrt(ms + NORM_EPS) * g_ref[...]
            o_ref[0, rows[ci], vs[h]] = (y * gate_ref[0, rows[ci], vs[h]].astype(F32)).astype(BF16)


def _gla(qb, kb, vb, la, gate, gla_g):
    B, S, _ = qb.shape
    tg = GLA_ROWS
    row3 = lambda width: pl.BlockSpec((1, tg, width), lambda b, i: (b, i, 0))
    return pl.pallas_call(
        _gla_kernel,
        grid=(B, S // tg),
        in_specs=[row3(GLA_QK_WIDTH), row3(GLA_QK_WIDTH), row3(GLA_WIDTH),
                  row3(GLA_QK_WIDTH), row3(GLA_WIDTH),
                  pl.BlockSpec((1, GLA_V_DIM), lambda b, i: (0, 0))],
        out_specs=row3(GLA_WIDTH),
        out_shape=jax.ShapeDtypeStruct((B, S, GLA_WIDTH), BF16),
        scratch_shapes=[pltpu.VMEM((GLA_HEADS, GLA_V_DIM, GLA_K_DIM), F32)],
        compiler_params=pltpu.CompilerParams(
            dimension_semantics=("parallel", "arbitrary"),
            vmem_limit_bytes=VMEM_LIMIT),
        name="gla",
    )(qb, kb, vb, la, gate, gla_g.reshape(1, GLA_V_DIM))


def _out_proj_kernel(oa_ref, ob_ref, w_ref, x_ref, g_ref, o_ref):
    y = (jnp.dot(oa_ref[0], w_ref[:DA_WIDTH, :], preferred_element_type=F32)
         + jnp.dot(ob_ref[0], w_ref[DA_WIDTH:, :], preferred_element_type=F32))
    ms = jnp.mean(y * y, axis=-1, keepdims=True)
    o_ref[0] = x_ref[0] + y * lax.rsqrt(ms + NORM_EPS) * g_ref[...]


def _out_proj(oa, ob, w_out, x, post_g):
    B, S, D = x.shape
    tm = OUT_ROWS
    row3 = lambda width: pl.BlockSpec((1, tm, width), lambda b, i: (b, i, 0))
    const = dict(pipeline_mode=pl.Buffered(1))
    return pl.pallas_call(
        _out_proj_kernel,
        grid=(B, S // tm),
        in_specs=[row3(DA_WIDTH), row3(GLA_WIDTH),
                  pl.BlockSpec(w_out.shape, lambda b, i: (0, 0), **const),
                  row3(D),
                  pl.BlockSpec((1, D), lambda b, i: (0, 0), **const)],
        out_specs=row3(D),
        out_shape=jax.ShapeDtypeStruct((B, S, D), F32),
        compiler_params=pltpu.CompilerParams(
            dimension_semantics=("parallel", "parallel"),
            vmem_limit_bytes=VMEM_LIMIT),
        name="out_proj",
    )(oa, ob, w_out.astype(BF16), x, post_g.reshape(1, D))


def kernel(x, pre_norm_g, post_norm_g, w_in, w_gk_up, b_gk, lambda_q1, lambda_k1,
           lambda_q2, lambda_k2, attn_subln_g, gla_norm_g, w_out):
    assert x.shape[-1] == D_MODEL and pre_norm_g.shape[0] == 1
    qt, vt, k, ga, qb, kb, vb, gb, la = _in_proj(
        x, pre_norm_g[0], w_in[0], w_gk_up[0], b_gk[0])
    oa = _diff_attn(qt, k, vt, ga, lambda_q1[0], lambda_k1[0], lambda_q2[0],
                    lambda_k2[0], attn_subln_g[0])
    ob = _gla(qb, kb, vb, la, gb, gla_norm_g[0])
    return _out_proj(oa, ob, w_out[0], x, post_norm_g[0])
```

```python
import math

import jax
import jax.numpy as jnp
from jax import lax
from jax.experimental import pallas as pl
from jax.experimental.pallas import tpu as pltpu

D_MODEL = 1024
DA_HEADS = 8
DA_HEAD_DIM = 64
DA_V_DIM = 128
DA_WIDTH = DA_HEADS * DA_V_DIM
GLA_HEADS = 4
GLA_K_DIM = 128
GLA_V_DIM = 256
GLA_QK_WIDTH = GLA_HEADS * GLA_K_DIM
GLA_WIDTH = GLA_HEADS * GLA_V_DIM
GLA_GATE_RANK = 16
GLA_GATE_NORMALIZER = 16.0
GLA_CHUNK = 64
NORM_EPS = 1e-6
LAM_INIT = 0.8 - 0.6 * math.exp(-0.3 * 0)
IN_COLS = dict(qa=0, ka=1024, va=2048, ga=3072, qb=4096, kb=4608, vb=5120, gb=6144, gk=7168)

LANE = 128
BF16_SUBLANES = 16
VMEM_LIMIT = 56 * 1024 * 1024

PROJ_ROWS = 512
ATT_TQ = 4096
ATT_TK = 256
ATT_VROWS = DA_V_DIM + BF16_SUBLANES
ATT_TN = 256
ATT_UNROLL = 4
GLA_ROWS = 512
OUT_ROWS = 1024
NEG_BIG = -1e30

BF16 = jnp.bfloat16
F32 = jnp.float32


def _silu(x):
    return x * (1.0 / (1.0 + jnp.exp(-x)))


def _log_sigmoid(x):
    return jnp.minimum(x, 0.0) - jnp.log(1.0 + jnp.exp(-jnp.abs(x)))


def _in_proj_kernel(x_ref, g_ref, w_ref, wgki_ref, wgk_ref, bgk_ref,
                    qt_ref, vt_ref, k_ref, ga_ref, qb_ref, kb_ref, vb_ref,
                    gb_ref, la_ref):
    x = x_ref[0]
    ms = jnp.mean(x * x, axis=-1, keepdims=True)
    h = (x * lax.rsqrt(ms + NORM_EPS) * g_ref[...]).astype(BF16)

    def proj(name, width):
        c0 = IN_COLS[name]
        return jnp.dot(h, w_ref[:, c0:c0 + width], preferred_element_type=F32)

    zq, zv = proj("qa", DA_WIDTH), proj("va", DA_WIDTH)
    q_scale = DA_HEAD_DIM ** -0.5 * math.log2(math.e)
    ones = jnp.ones((BF16_SUBLANES, ATT_TK), BF16)
    for hh in range(DA_HEADS):
        cs = slice(hh * LANE, (hh + 1) * LANE)
        for blk in range(x.shape[0] // ATT_TK):
            rs = slice(blk * ATT_TK, (blk + 1) * ATT_TK)
            qt_ref[0, hh, blk] = (zq[rs, cs].T * q_scale).astype(BF16)
            vt_ref[0, hh, blk, :DA_V_DIM, :] = zv[rs, cs].T.astype(BF16)
            vt_ref[0, hh, blk, DA_V_DIM:, :] = ones

    zk = proj("ka", DA_WIDTH)
    for hh in range(DA_HEADS):
        k_ref[0, hh] = zk[:, hh * LANE:(hh + 1) * LANE].astype(BF16)
    ga_ref[0] = _silu(proj("ga", DA_WIDTH)).astype(BF16)
    zqk = proj("qb", 2 * GLA_QK_WIDTH)
    qb_ref[0] = (zqk[:, :GLA_QK_WIDTH] * (GLA_K_DIM ** -0.5)).astype(BF16)
    kb_ref[0] = zqk[:, GLA_QK_WIDTH:].astype(BF16)
    vb_ref[0] = proj("vb", GLA_WIDTH).astype(BF16)
    gb_ref[0] = _silu(proj("gb", GLA_WIDTH)).astype(BF16)
    gk_low = jnp.dot(h, wgki_ref[...], preferred_element_type=F32).astype(BF16)
    gk = jnp.dot(gk_low, wgk_ref[...], preferred_element_type=F32) + bgk_ref[...]
    la_ref[0] = _log_sigmoid(gk) * (math.log2(math.e) / GLA_GATE_NORMALIZER)


def _in_proj(x, pre_g, w_in, w_gk_up, b_gk):
    B, S, D = x.shape
    tm = PROJ_ROWS
    nblk = tm // ATT_TK

    def sl(w, name, width):
        return w[:, IN_COLS[name]:IN_COLS[name] + width]

    w = w_in.astype(BF16)
    wgki = jnp.pad(sl(w, "gk", GLA_GATE_RANK), ((0, 0), (0, LANE - GLA_GATE_RANK)))
    wgk = jnp.pad(w_gk_up, ((0, LANE - GLA_GATE_RANK), (0, 0))).astype(BF16)
    bgk = b_gk.reshape(1, GLA_QK_WIDTH)
    g = pre_g.reshape(1, D)

    const = dict(pipeline_mode=pl.Buffered(1))
    in_specs = [
        pl.BlockSpec((1, tm, D), lambda b, i: (b, i, 0)),
        pl.BlockSpec((1, D), lambda b, i: (0, 0), **const),
        pl.BlockSpec(w.shape, lambda b, i: (0, 0), **const),
        pl.BlockSpec(wgki.shape, lambda b, i: (0, 0), **const),
        pl.BlockSpec(wgk.shape, lambda b, i: (0, 0), **const),
        pl.BlockSpec(bgk.shape, lambda b, i: (0, 0), **const),
    ]
    row3 = lambda width: pl.BlockSpec((1, tm, width), lambda b, i: (b, i, 0))
    out_specs = [
        pl.BlockSpec((1, DA_HEADS, nblk, LANE, ATT_TK), lambda b, i: (b, 0, i, 0, 0)),
        pl.BlockSpec((1, DA_HEADS, nblk, ATT_VROWS, ATT_TK), lambda b, i: (b, 0, i, 0, 0)),
        pl.BlockSpec((1, DA_HEADS, tm, LANE), lambda b, i: (b, 0, i, 0)),
        row3(DA_WIDTH),
        row3(GLA_QK_WIDTH), row3(GLA_QK_WIDTH), row3(GLA_WIDTH), row3(GLA_WIDTH),
        row3(GLA_QK_WIDTH),
    ]
    out_shape = [
        jax.ShapeDtypeStruct((B, DA_HEADS, S // ATT_TK, LANE, ATT_TK), BF16),
        jax.ShapeDtypeStruct((B, DA_HEADS, S // ATT_TK, ATT_VROWS, ATT_TK), BF16),
        jax.ShapeDtypeStruct((B, DA_HEADS, S, LANE), BF16),
        jax.ShapeDtypeStruct((B, S, DA_WIDTH), BF16),
        jax.ShapeDtypeStruct((B, S, GLA_QK_WIDTH), BF16),
        jax.ShapeDtypeStruct((B, S, GLA_QK_WIDTH), BF16),
        jax.ShapeDtypeStruct((B, S, GLA_WIDTH), BF16),
        jax.ShapeDtypeStruct((B, S, GLA_WIDTH), BF16),
        jax.ShapeDtypeStruct((B, S, GLA_QK_WIDTH), F32),
    ]
    return pl.pallas_call(
        _in_proj_kernel,
        grid=(B, S // tm),
        in_specs=in_specs,
        out_specs=out_specs,
        out_shape=out_shape,
        compiler_params=pltpu.CompilerParams(
            dimension_semantics=("parallel", "parallel"),
            vmem_limit_bytes=VMEM_LIMIT),
        name="in_proj",
    )(x, g, w, wgki, wgk, bgk)


def _diff_attn_kernel(qt_ref, k_ref, vt_ref, gate_ref, lq1_ref, lk1_ref, lq2_ref,
                      lk2_ref, g_ref, o_ref, acc_ref, m_ref, s_ref, c_ref):
    tq, tk, tn = ATT_TQ, ATT_TK, ATT_TN
    n_tiles = tq // tn
    sub = tq // tk
    n_q = o_ref.shape[1] // tq
    groups = [(mp, n) for mp in range(2) for n in range(n_tiles)]
    all_groups = [(g, False) for g in range(len(groups))]
    row = lax.broadcasted_iota(jnp.int32, (LANE, tn), 0)
    map_rows = (row < DA_HEAD_DIM, row >= DA_HEAD_DIM)
    causal = (lax.broadcasted_iota(jnp.int32, (tk, tn), 0)
              <= lax.broadcasted_iota(jnp.int32, (tk, tn), 1))
    lam = (jnp.exp(jnp.sum(lq1_ref[...] * lk1_ref[...], axis=-1, keepdims=True))
           - jnp.exp(jnp.sum(lq2_ref[...] * lk2_ref[...], axis=-1, keepdims=True))
           + LAM_INIT)
    gain = jnp.broadcast_to(g_ref[...] * (1.0 - LAM_INIT), (DA_V_DIM, tn))

    def aligned(start, size):
        return pl.ds(start if isinstance(start, int) else pl.multiple_of(start, size), size)

    def key_tile(j):
        return k_ref[0, 0, aligned(j * tk, tk), :]

    def epilogue(qi, n):
        a1, a2 = acc_ref[n], acc_ref[n_tiles + n]
        l1, l2 = a1[DA_V_DIM:DA_V_DIM + 1], a2[DA_V_DIM:DA_V_DIM + 1]
        o = a1[:DA_V_DIM] * (1.0 / l1) - a2[:DA_V_DIM] * (lam / l2)
        msq = jnp.mean(o * o, axis=0, keepdims=True)
        y = o * lax.rsqrt(msq + NORM_EPS) * gain
        rows = aligned(qi * tq + n * tn, tn)
        o_ref[0, rows, :] = (y.T * gate_ref[0, rows, :].astype(F32)).astype(BF16)

    def q_weights(qi):
        out = []
        for mp, n in groups:
            blk = qt_ref[0, 0, qi * n_tiles + n]
            out.append(jnp.where(map_rows[mp], blk, jnp.zeros_like(blk)))
        return out

    def issue(qw, k, g, slot, masked):
        s = jnp.dot(k, qw[g], preferred_element_type=F32)
        if masked:
            s = jnp.where(causal, s, NEG_BIG)
        s_ref[slot, g] = s
        c_ref[slot, g] = jnp.max(s, axis=0, keepdims=True)

    def reset_stats():
        acc_ref[...] = jnp.zeros_like(acc_ref)
        m_ref[...] = jnp.full_like(m_ref, NEG_BIG)

    def query_tile(qi, carry):
        qw = q_weights(qi)
        qw_next = q_weights(jnp.minimum(qi + 1, n_q - 1))

        def consume(vt, g, slot, mask_now):
            s, c = s_ref[slot, g], c_ref[slot, g]
            if mask_now:
                s = jnp.where(causal, s, NEG_BIG)
                c = jnp.max(s, axis=0, keepdims=True)
            m_old = m_ref[g]
            m_new = jnp.maximum(m_old, c)
            alpha = jnp.exp2(m_old - m_new)
            p = jnp.exp2(s - m_new).astype(BF16)
            acc_ref[g] = alpha * acc_ref[g] + jnp.dot(vt, p, preferred_element_type=F32)
            m_ref[g] = m_new

        def step(issues, vt, cur, slot):
            for idx in range(max(len(issues), len(cur))):
                if idx < len(issues):
                    issue(*issues[idx])
                if idx < len(cur):
                    consume(vt, cur[idx][0], slot, cur[idx][1])

        n_full = qi * sub

        def body(i, carry):
            for t in range(ATT_UNROLL):
                j = ATT_UNROLL * i + t
                k_next = key_tile(j + 1)
                step([(qw, k_next, g, 1 - t % 2, False) for g, _ in all_groups],
                     vt_ref[0, 0, j], all_groups, t % 2)
            return carry

        lax.fori_loop(0, n_full // ATT_UNROLL, body, 0)

        def diag_groups(jj):
            return [(g, n == jj) for g, (_, n) in enumerate(groups) if n >= jj]

        k_zero = key_tile(0)
        for jj in range(sub):
            slot = jj % 2
            cur = diag_groups(jj) if jj == 0 else [(g, False) for g, _ in diag_groups(jj)]
            issues = []
            if jj + 1 < sub:
                k_next = key_tile(n_full + jj + 1)
                issues += [(qw, k_next, g, 1 - slot, masked) for g, masked in diag_groups(jj + 1)]
            finished = [g for g, (_, n) in enumerate(groups)
                        if n == jj - 1 or (jj == sub - 1 and n == jj)]
            issues += [(qw_next, k_zero, g, 0, False) for g in finished]
            step(issues, vt_ref[0, 0, n_full + jj], cur, slot)
            epilogue(qi, jj)
        reset_stats()
        return carry

    k_first = key_tile(0)
    qw_first = q_weights(0)
    for g, _ in all_groups:
        issue(qw_first, k_first, g, 0, False)
    reset_stats()
    lax.fori_loop(0, n_q, query_tile, 0)


def _diff_attn(qt, k, vt, gate, lq1, lk1, lq2, lk2, subln_g):
    B, H, S, _ = k.shape
    tq, tk, tn = ATT_TQ, ATT_TK, ATT_TN
    assert tk == tn and ATT_UNROLL % 2 == 0 and (tq // tk) % ATT_UNROLL == 0
    n_col_groups = 2 * (tq // tn)
    lam_spec = pl.BlockSpec((1, DA_HEAD_DIM), lambda b, h: (0, 0))
    return pl.pallas_call(
        _diff_attn_kernel,
        grid=(B, H),
        in_specs=[
            pl.BlockSpec((1, 1, S // tn, LANE, tn), lambda b, h: (b, h, 0, 0, 0)),
            pl.BlockSpec((1, 1, S, LANE), lambda b, h: (b, h, 0, 0)),
            pl.BlockSpec((1, 1, S // tk, ATT_VROWS, tk), lambda b, h: (b, h, 0, 0, 0)),
            pl.BlockSpec((1, S, LANE), lambda b, h: (b, 0, h)),
            lam_spec, lam_spec, lam_spec, lam_spec,
            pl.BlockSpec((DA_V_DIM, 1), lambda b, h: (0, 0)),
        ],
        out_specs=pl.BlockSpec((1, S, LANE), lambda b, h: (b, 0, h)),
        out_shape=jax.ShapeDtypeStruct((B, S, DA_WIDTH), BF16),
        scratch_shapes=[
            pltpu.VMEM((n_col_groups, ATT_VROWS, tn), F32),
            pltpu.VMEM((n_col_groups, 1, tn), F32),
            pltpu.VMEM((2, n_col_groups, tk, tn), F32),
            pltpu.VMEM((2, n_col_groups, 1, tn), F32),
        ],
        compiler_params=pltpu.CompilerParams(
            dimension_semantics=("parallel", "parallel"),
            vmem_limit_bytes=VMEM_LIMIT),
        name="diff_attn",
    )(qt, k, vt, gate, lq1.reshape(1, -1), lk1.reshape(1, -1), lq2.reshape(1, -1),
      lk2.reshape(1, -1), subln_g.reshape(DA_V_DIM, 1))


def _gla_kernel(q_ref, k_ref, v_ref, la_ref, gate_ref, g_ref, o_ref, state_ref):
    C = GLA_CHUNK

    @pl.when(pl.program_id(1) == 0)
    def _():
        state_ref[...] = jnp.zeros_like(state_ref)

    r = lax.broadcasted_iota(jnp.int32, (C, C), 0)
    c = lax.broadcasted_iota(jnp.int32, (C, C), 1)
    causal = c <= r
    tri = jnp.where(causal, 1.0, 0.0).astype(BF16)
    tn = (((0,), (0,)), ((), ()))
    nt = (((1,), (1,)), ((), ()))
    chunks = range(GLA_ROWS // C)
    heads = range(GLA_HEADS)
    items = [(ci, h) for ci in chunks for h in heads]
    rows = {ci: slice(ci * C, (ci + 1) * C) for ci in chunks}
    ks = {h: slice(h * GLA_K_DIM, (h + 1) * GLA_K_DIM) for h in heads}
    vs = {h: slice(h * GLA_V_DIM, (h + 1) * GLA_V_DIM) for h in heads}

    b = {}
    for ci, h in items:
        la = la_ref[0, rows[ci], ks[h]]
        la_hi = la.astype(BF16)
        la_lo = (la - la_hi.astype(F32)).astype(BF16)
        b[ci, h] = (jnp.dot(tri, la_hi, preferred_element_type=F32)
                    + jnp.dot(tri, la_lo, preferred_element_type=F32))
    v = {(ci, h): v_ref[0, rows[ci], vs[h]] for ci, h in items}
    q_dec, k_inv, k_tail, decay = {}, {}, {}, {}
    for ci, h in items:
        bb = b[ci, h]
        b_last = bb[C - 1:C, :]
        q = q_ref[0, rows[ci], ks[h]].astype(F32)
        k = k_ref[0, rows[ci], ks[h]].astype(F32)
        q_dec[ci, h] = (q * jnp.exp2(bb)).astype(BF16)
        k_inv[ci, h] = (k * jnp.exp2(-bb)).astype(BF16)
        k_tail[ci, h] = (k * jnp.exp2(b_last - bb)).astype(BF16)
        decay[ci, h] = jnp.exp2(b_last)
    attn = {i: jnp.where(causal, lax.dot_general(q_dec[i], k_inv[i], nt,
                                                 preferred_element_type=F32), 0.0).astype(BF16)
            for i in items}
    kv_t = {i: lax.dot_general(v[i], k_tail[i], tn, preferred_element_type=F32)
            for i in items}

    for ci in chunks:
        o = {}
        for h in heads:
            state_t = state_ref[h]
            o[h] = (jnp.dot(attn[ci, h], v[ci, h], preferred_element_type=F32)
                    + lax.dot_general(q_dec[ci, h], state_t.astype(BF16), nt,
                                      preferred_element_type=F32))
            state_ref[h] = state_t * decay[ci, h] + kv_t[ci, h]
        for h in heads:
            ms = jnp.mean(o[h] * o[h], axis=-1, keepdims=True)
            y = o[h] * lax.rsqrt(ms + NORM_EPS) * g_ref[...]
            o_ref[0, rows[ci], vs[h]] = (y * gate_ref[0, rows[ci], vs[h]].astype(F32)).astype(BF16)


def _gla(qb, kb, vb, la, gate, gla_g):
    B, S, _ = qb.shape
    tg = GLA_ROWS
    row3 = lambda width: pl.BlockSpec((1, tg, width), lambda b, i: (b, i, 0))
    return pl.pallas_call(
        _gla_kernel,
        grid=(B, S // tg),
        in_specs=[row3(GLA_QK_WIDTH), row3(GLA_QK_WIDTH), row3(GLA_WIDTH),
                  row3(GLA_QK_WIDTH), row3(GLA_WIDTH),
                  pl.BlockSpec((1, GLA_V_DIM), lambda b, i: (0, 0))],
        out_specs=row3(GLA_WIDTH),
        out_shape=jax.ShapeDtypeStruct((B, S, GLA_WIDTH), BF16),
        scratch_shapes=[pltpu.VMEM((GLA_HEADS, GLA_V_DIM, GLA_K_DIM), F32)],
        compiler_params=pltpu.CompilerParams(
            dimension_semantics=("parallel", "arbitrary"),
            vmem_limit_bytes=VMEM_LIMIT),
        name="gla",
    )(qb, kb, vb, la, gate, gla_g.reshape(1, GLA_V_DIM))


def _out_proj_kernel(oa_ref, ob_ref, w_ref, x_ref, g_ref, o_ref):
    y = (jnp.dot(oa_ref[0], w_ref[:DA_WIDTH, :], preferred_element_type=F32)
         + jnp.dot(ob_ref[0], w_ref[DA_WIDTH:, :], preferred_element_type=F32))
    ms = jnp.mean(y * y, axis=-1, keepdims=True)
    o_ref[0] = x_ref[0] + y * lax.rsqrt(ms + NORM_EPS) * g_ref[...]


def _out_proj(oa, ob, w_out, x, post_g):
    B, S, D = x.shape
    tm = OUT_ROWS
    row3 = lambda width: pl.BlockSpec((1, tm, width), lambda b, i: (b, i, 0))
    const = dict(pipeline_mode=pl.Buffered(1))
    return pl.pallas_call(
        _out_proj_kernel,
        grid=(B, S // tm),
        in_specs=[row3(DA_WIDTH), row3(GLA_WIDTH),
                  pl.BlockSpec(w_out.shape, lambda b, i: (0, 0), **const),
                  row3(D),
                  pl.BlockSpec((1, D), lambda b, i: (0, 0), **const)],
        out_specs=row3(D),
        out_shape=jax.ShapeDtypeStruct((B, S, D), F32),
        compiler_params=pltpu.CompilerParams(
            dimension_semantics=("parallel", "parallel"),
            vmem_limit_bytes=VMEM_LIMIT),
        name="out_proj",
    )(oa, ob, w_out.astype(BF16), x, post_g.reshape(1, D))


def kernel(x, pre_norm_g, post_norm_g, w_in, w_gk_up, b_gk, lambda_q1, lambda_k1,
           lambda_q2, lambda_k2, attn_subln_g, gla_norm_g, w_out):
    assert x.shape[-1] == D_MODEL and pre_norm_g.shape[0] == 1
    qt, vt, k, ga, qb, kb, vb, gb, la = _in_proj(
        x, pre_norm_g[0], w_in[0], w_gk_up[0], b_gk[0])
    oa = _diff_attn(qt, k, vt, ga, lambda_q1[0], lambda_k1[0], lambda_q2[0],
                    lambda_k2[0], attn_subln_g[0])
    ob = _gla(qb, kb, vb, la, gb, gla_norm_g[0])
    return _out_proj(oa, ob, w_out[0], x, post_norm_g[0])
```
